```python
import jax, jax.numpy as jnp
from jax import lax
import numpy as np

D_MODEL = 4096
BATCH = 32
SEQ = 256
DEPTH = 4
DEC_BATCH = 4
DEC_SEQ = 2048
PAST_LEN = 512

GRID_W = 64
HEAD_DIM = 128
ATTN_HEADS = D_MODEL // 2 // HEAD_DIM
ATTN_KV_HEADS = ATTN_HEADS // 4
ATTN_WIDTH = ATTN_HEADS * HEAD_DIM
KV_WIDTH = ATTN_KV_HEADS * HEAD_DIM
GLA_DV = 256
GLA_DK = GLA_DV // 2
GLA_HEADS = D_MODEL // 2 // GLA_DV
GLA_KEY_WIDTH = GLA_HEADS * GLA_DK
GLA_WIDTH = GLA_HEADS * GLA_DV
GLA_RANK = 16
GLA_TAU = 16.0
GLA_CHUNK = 64
Q_BLOCK = 128
ROPE_THETA = 10000.0
ROPE_AXIS_DIM = HEAD_DIM // 2
D_FF = 2 * D_MODEL
N_EXPERTS = 8
TOP_K = 2
D_FF_EXPERT = D_MODEL // 2
N_DENSE = (DEPTH + 1) // 2
N_MOE = DEPTH // 2
N_MOD = 6
EPS = 1e-6
IN_SIZES = (ATTN_WIDTH, KV_WIDTH, KV_WIDTH, GLA_KEY_WIDTH, GLA_KEY_WIDTH, GLA_WIDTH, GLA_WIDTH, 2 * GLA_RANK)
IN_COLS = sum(IN_SIZES)
IN_SPLITS = tuple(int(s) for s in np.cumsum(IN_SIZES)[:-1])

kernel_name = 'hybrid_gla_gqa_prefix_diffusion_step'


def rms_norm(x, w):
    xf = x.astype(jnp.float32)
    y = xf * lax.rsqrt(jnp.mean(xf * xf, axis=-1, keepdims=True) + EPS)
    return y.astype(x.dtype) * w


def ada_modulation(cond, w_ada_l, b_ada_l):
    mod = (jax.nn.silu(cond) @ w_ada_l + b_ada_l)[:, None, :]
    return jnp.split(mod, N_MOD, axis=-1)


def rope_tables(n_tok):
    rows = n_tok // GRID_W
    row = jnp.repeat(jnp.arange(rows), GRID_W).astype(jnp.float32)
    col = jnp.tile(jnp.arange(GRID_W), rows).astype(jnp.float32)
    inv = ROPE_THETA ** (-jnp.arange(ROPE_AXIS_DIM // 2, dtype=jnp.float32) * 2.0 / ROPE_AXIS_DIM)
    ang_r = row[:, None] * inv
    ang_c = col[:, None] * inv
    return (jnp.cos(ang_r)[:, None, :], jnp.sin(ang_r)[:, None, :],
            jnp.cos(ang_c)[:, None, :], jnp.sin(ang_c)[:, None, :])


def _rotate(x, cos, sin):
    x1, x2 = jnp.split(x, 2, axis=-1)
    cos = cos.astype(x.dtype)
    sin = sin.astype(x.dtype)
    return jnp.concatenate([x1 * cos - x2 * sin, x2 * cos + x1 * sin], axis=-1)


def apply_axial_rope(x, tables):
    cos_r, sin_r, cos_c, sin_c = tables
    x_row, x_col = jnp.split(x, 2, axis=-1)
    return jnp.concatenate([_rotate(x_row, cos_r, sin_r), _rotate(x_col, cos_c, sin_c)], axis=-1)


def block_attention(q, k, v):
    b, t, h, d = q.shape
    n_kv = k.shape[2]
    g = h // n_kv
    nb = t // Q_BLOCK
    scale = d ** -0.5
    qb = q.reshape(b, nb, Q_BLOCK, n_kv, g, d).transpose(1, 0, 2, 3, 4, 5)

    def one_block(q_blk):
        s = jnp.einsum('bqkgd,bskd->bkgqs', q_blk, k, preferred_element_type=jnp.float32) * scale
        p = jax.nn.softmax(s, axis=-1).astype(v.dtype)
        return jnp.einsum('bkgqs,bskd->bqkgd', p, v)

    o = lax.map(one_block, qb)
    return o.transpose(1, 0, 2, 3, 4, 5).reshape(b, t, h * d)


def gla_chunk_scan(q, k, v, log_a, s0):
    b, t, h, _ = q.shape
    n = t // GLA_CHUNK

    def chunks(a):
        return a.reshape(b, n, GLA_CHUNK, h, a.shape[-1]).transpose(0, 3, 1, 2, 4)

    q, k, v, log_a = chunks(q), chunks(k), chunks(v), chunks(log_a)
    cum = jnp.cumsum(log_a, axis=3)
    cum_last = cum[:, :, :, -1:, :]
    q_dec = q * jnp.exp(cum)
    k_inv = k * jnp.exp(-cum)
    k_to_end = k * jnp.exp(cum_last - cum)
    mask = jnp.tril(jnp.ones((GLA_CHUNK, GLA_CHUNK), dtype=bool))
    att = jnp.where(mask, jnp.einsum('bhncd,bhnsd->bhncs', q_dec, k_inv), 0.0)
    o_intra = jnp.einsum('bhncs,bhnse->bhnce', att, v)
    kv = jnp.einsum('bhncd,bhnce->nbhde', k_to_end, v)
    decay = jnp.exp(cum_last[:, :, :, 0, :]).transpose(2, 0, 1, 3)

    def step(s, inp):
        d_c, kv_c = inp
        return d_c[..., None] * s + kv_c, s

    s_final, s_prev = lax.scan(step, s0, (decay, kv))
    o_inter = jnp.einsum('bhncd,nbhde->bhnce', q_dec, s_prev)
    o = (o_intra + o_inter).transpose(0, 2, 3, 1, 4).reshape(b, t, h, v.shape[-1])
    return o, s_final


def gla_mixer(qg, kg, vg, gg, lr, gla_up_l, gla_bias_l, gla_norm_w_l, s0_f, s0_b):
    b, t, _ = qg.shape
    f32 = jnp.float32
    q = qg.astype(f32).reshape(b, t, GLA_HEADS, GLA_DK) * GLA_DK ** -0.5
    k = kg.astype(f32).reshape(b, t, GLA_HEADS, GLA_DK)
    v = vg.astype(f32).reshape(b, t, GLA_HEADS, GLA_DV)
    lr_f, lr_b = jnp.split(lr.astype(f32), 2, axis=-1)

    def log_decay(lr_dir, d):
        logits = lr_dir @ gla_up_l[d].astype(f32) + gla_bias_l[d].astype(f32)
        return (jax.nn.log_sigmoid(logits) / GLA_TAU).reshape(b, t, GLA_HEADS, GLA_DK)

    def flip(a):
        return jnp.flip(a, axis=1)

    o_f, s_f = gla_chunk_scan(q, k, v, log_decay(lr_f, 0), s0_f)
    o_b, s_b = gla_chunk_scan(flip(q), flip(k), flip(v), flip(log_decay(lr_b, 1)), s0_b)
    o = rms_norm(o_f + flip(o_b), gla_norm_w_l.astype(f32))
    out = o.reshape(b, t, GLA_WIDTH).astype(gg.dtype) * jax.nn.silu(gg)
    return out, s_f, s_b


def dense_swiglu(u, w1, w3, w2):
    return (jax.nn.silu(u @ w1) * (u @ w3)) @ w2


def moe_swiglu(u, router_w, w1, w3, w2):
    b, t, d = u.shape
    xt = u.reshape(b * t, d)
    logits = jnp.einsum('nd,de->ne', xt, router_w, preferred_element_type=jnp.float32)
    top_val, top_idx = lax.top_k(logits, TOP_K)
    gates = jax.nn.softmax(top_val, axis=-1)
    combine = jnp.einsum('nk,nke->ne', gates,
                         jax.nn.one_hot(top_idx, N_EXPERTS, dtype=jnp.float32)).astype(u.dtype)
    hid = jax.nn.silu(jnp.einsum('nd,edf->nef', xt, w1)) * jnp.einsum('nd,edf->nef', xt, w3)
    y = jnp.einsum('nef,efd->nd', hid * combine[:, :, None], w2)
    return y.reshape(b, t, d)


def trunk_layer(h, cond, l, weights, rope, ctx_k, ctx_v, s0_f, s0_b):
    (norm1_w, norm2_w, w_ada, b_ada, w_in, q_norm_w, k_norm_w, gla_up, gla_bias, gla_norm_w, w_out,
     ffn_w1, ffn_w3, ffn_w2, router_w, moe_w1, moe_w3, moe_w2) = weights
    b, t, _ = h.shape
    sh1, sc1, g1, sh2, sc2, g2 = ada_modulation(cond, w_ada[l], b_ada[l])
    u = rms_norm(h, norm1_w[l]) * (1 + sc1) + sh1
    z = u @ w_in[l]
    qa, ka, va, qg, kg, vg, gg, lr = jnp.split(z, IN_SPLITS, axis=-1)
    qa = rms_norm(qa.reshape(b, t, ATTN_HEADS, HEAD_DIM), q_norm_w[l])
    ka = rms_norm(ka.reshape(b, t, ATTN_KV_HEADS, HEAD_DIM), k_norm_w[l])
    va = va.reshape(b, t, ATTN_KV_HEADS, HEAD_DIM)
    if rope is None:
        attn = block_attention(qa, ka, va)
    else:
        qr = apply_axial_rope(qa, rope)
        kr = apply_axial_rope(ka, rope)
        keys = jnp.concatenate([kr, ctx_k.astype(kr.dtype)], axis=1)
        vals = jnp.concatenate([va, ctx_v.astype(va.dtype)], axis=1)
        attn = block_attention(qr, keys, vals)
    gla, s_f, s_b = gla_mixer(qg, kg, vg, gg, lr, gla_up[l], gla_bias[l], gla_norm_w[l], s0_f, s0_b)
    mix = jnp.concatenate([attn, gla], axis=-1) @ w_out[l]
    h = h + g1 * mix
    u = rms_norm(h, norm2_w[l]) * (1 + sc2) + sh2
    i = l // 2
    if l % 2 == 0:
        ffn = dense_swiglu(u, ffn_w1[i], ffn_w3[i], ffn_w2[i])
    else:
        ffn = moe_swiglu(u, router_w[i], moe_w1[i], moe_w3[i], moe_w2[i])
    h = h + g2 * ffn
    return h, ka, va, s_f, s_b


def setup_inputs(seed: int = 0) -> dict:
    key = jax.random.key(seed)
    ks = jax.random.split(key, 26)
    f32 = jnp.float32

    def nrm(k, shape, scale=1.0):
        return jax.random.normal(k, shape, f32) * scale

    return {
        'x_prompt': nrm(ks[0], (BATCH, SEQ, D_MODEL)),
        'x_sample': nrm(ks[1], (DEC_BATCH, DEC_SEQ, D_MODEL)),
        'cache_k': nrm(ks[2], (DEC_BATCH, DEPTH, PAST_LEN, ATTN_KV_HEADS, HEAD_DIM)),
        'cache_v': nrm(ks[3], (DEC_BATCH, DEPTH, PAST_LEN, ATTN_KV_HEADS, HEAD_DIM)),
        'state_gla': nrm(ks[4], (DEC_BATCH, DEPTH, 2, GLA_HEADS, GLA_DK, GLA_DV)),
        'c': nrm(ks[5], (DEC_BATCH, D_MODEL)),
        'c_ctx': nrm(ks[6], (D_MODEL,)),
        'norm1_w': 1.0 + nrm(ks[7], (DEPTH, D_MODEL), 0.02),
        'norm2_w': 1.0 + nrm(ks[8], (DEPTH, D_MODEL), 0.02),
        'w_ada': nrm(ks[9], (DEPTH, D_MODEL, N_MOD * D_MODEL), 0.5 * D_MODEL ** -0.5),
        'b_ada': nrm(ks[10], (DEPTH, N_MOD * D_MODEL), 0.02),
        'w_in': nrm(ks[11], (DEPTH, D_MODEL, IN_COLS), D_MODEL ** -0.5),
        'q_norm_w': 1.0 + nrm(ks[12], (DEPTH, HEAD_DIM), 0.02),
        'k_norm_w': 1.0 + nrm(ks[13], (DEPTH, HEAD_DIM), 0.02),
        'gla_up': nrm(ks[14], (DEPTH, 2, GLA_RANK, GLA_KEY_WIDTH), GLA_RANK ** -0.5),
        'gla_bias': nrm(ks[15], (DEPTH, 2, GLA_KEY_WIDTH), 0.1),
        'gla_norm_w': 1.0 + nrm(ks[16], (DEPTH, GLA_DV), 0.02),
        'w_out': nrm(ks[17], (DEPTH, D_MODEL, D_MODEL), D_MODEL ** -0.5),
        'ffn_w1': nrm(ks[18], (N_DENSE, D_MODEL, D_FF), D_MODEL ** -0.5),
        'ffn_w3': nrm(ks[19], (N_DENSE, D_MODEL, D_FF), D_MODEL ** -0.5),
        'ffn_w2': nrm(ks[20], (N_DENSE, D_FF, D_MODEL), D_FF ** -0.5),
        'router_w': nrm(ks[21], (N_MOE, D_MODEL, N_EXPERTS), D_MODEL ** -0.5),
        'moe_w1': nrm(ks[22], (N_MOE, N_EXPERTS, D_MODEL, D_FF_EXPERT), D_MODEL ** -0.5),
        'moe_w3': nrm(ks[23], (N_MOE, N_EXPERTS, D_MODEL, D_FF_EXPERT), D_MODEL ** -0.5),
        'moe_w2': nrm(ks[24], (N_MOE, N_EXPERTS, D_FF_EXPERT, D_MODEL), D_FF_EXPERT ** -0.5),
    }


def reference(x_prompt, x_sample, cache_k, cache_v, state_gla, c, c_ctx, norm1_w, norm2_w, w_ada, b_ada,
              w_in, q_norm_w, k_norm_w, gla_up, gla_bias, gla_norm_w, w_out, ffn_w1, ffn_w3, ffn_w2,
              router_w, moe_w1, moe_w3, moe_w2):
    weights = (norm1_w, norm2_w, w_ada, b_ada, w_in, q_norm_w, k_norm_w, gla_up, gla_bias, gla_norm_w, w_out,
               ffn_w1, ffn_w3, ffn_w2, router_w, moe_w1, moe_w3, moe_w2)
    zero_state = jnp.zeros((x_prompt.shape[0], GLA_HEADS, GLA_DK, GLA_DV), jnp.float32)
    h = x_prompt
    ks, vs, ss = [], [], []
    for l in range(DEPTH):
        h, k_l, v_l, s_f, s_b = trunk_layer(h, c_ctx[None, :], l, weights, None, None, None,
                                            zero_state, zero_state)
        ks.append(k_l)
        vs.append(v_l)
        ss.append(jnp.stack([s_f, s_b], axis=1))
    y_prompt = h
    new_cache_k = jnp.stack(ks, axis=1)
    new_cache_v = jnp.stack(vs, axis=1)
    new_state_gla = jnp.stack(ss, axis=1).astype(x_prompt.dtype)
    rope = rope_tables(x_sample.shape[1])
    h = x_sample
    for l in range(DEPTH):
        h, _, _, _, _ = trunk_layer(h, c, l, weights, rope, cache_k[:, l], cache_v[:, l],
                                    state_gla[:, l, 0].astype(jnp.float32),
                                    state_gla[:, l, 1].astype(jnp.float32))
    y_sample = h
    return (y_prompt, y_sample, new_cache_k, new_cache_v, new_state_gla)
```

```python
import functools
import math

import jax
import jax.numpy as jnp
from jax import lax
from jax.experimental import pallas as pl
from jax.experimental.pallas import tpu as pltpu

F32 = jnp.float32
BF16 = jnp.bfloat16

EPS = 1e-6
GRID_W = 64
GLA_CHUNK = 64
GLA_TAU = 16.0
ROPE_THETA = 10000.0
N_MOD = 6
LANES = 128
SUBLANES = 8
VMEM_LIMIT_BYTES = 56 * 1024 * 1024


def _params(*semantics):
    return pltpu.CompilerParams(dimension_semantics=semantics, vmem_limit_bytes=VMEM_LIMIT_BYTES)


def _tile(n, cap):
    t = cap
    while t > 1 and n % t:
        t //= 2
    return t


def _dot(a, b):
    return jnp.dot(a, b, preferred_element_type=F32)


def _dot_nt(a, b):
    return lax.dot_general(a, b, (((1,), (1,)), ((), ())), preferred_element_type=F32)


def _dot_tn(a, b):
    return lax.dot_general(a, b, (((0,), (0,)), ((), ())), preferred_element_type=F32)


def _split_bf16(x):
    hi = x.astype(BF16)
    lo = (x - hi.astype(F32)).astype(BF16)
    return hi, lo


def _silu(x):
    return x * jax.nn.sigmoid(x)


def _ada_kernel(c_ref, w_ref, b_ref, o_ref):
    s = _silu(c_ref[...]).astype(BF16)
    o_ref[...] = _dot(s, w_ref[...].astype(BF16)) + b_ref[...]


def ada_modulation(cond, w_ada, b_ada):
    depth, d, n = w_ada.shape
    r = cond.shape[0]
    tn = _tile(n, 1024)
    return pl.pallas_call(
        _ada_kernel,
        grid=(depth, n // tn),
        in_specs=[
            pl.BlockSpec((r, d), lambda l, j: (0, 0)),
            pl.BlockSpec((None, d, tn), lambda l, j: (l, 0, j)),
            pl.BlockSpec((None, 1, tn), lambda l, j: (l, 0, j)),
        ],
        out_specs=pl.BlockSpec((None, r, tn), lambda l, j: (l, 0, j)),
        out_shape=jax.ShapeDtypeStruct((depth, r, n), F32),
        compiler_params=_params("arbitrary", "arbitrary"),
        name="ada",
    )(cond, w_ada, b_ada.reshape(depth, 1, n))


def _norm_kernel(h_ref, nw_ref, sc_ref, sh_ref, *rest, mode, n_experts):
    x = h_ref[...]
    ms = jnp.mean(x * x, axis=-1, keepdims=True)
    y = x * lax.rsqrt(ms + EPS) * nw_ref[...]
    u = y * (1.0 + sc_ref[...]) + sh_ref[...]
    if mode == "decay":
        wlr_ref, up_ref, bias_ref, u_ref, la_ref = rest
        ub = u.astype(BF16)
        u_ref[...] = ub
        lr = _dot(ub, wlr_ref[...])
        lr_hi, lr_lo = _split_bf16(lr)
        up_hi, up_lo = _split_bf16(up_ref[...])
        logits = _dot(lr_hi, up_hi) + (_dot(lr_lo, up_hi) + _dot(lr_hi, up_lo)) + bias_ref[...]
        log_sig = jnp.minimum(logits, 0.0) - jnp.log1p(jnp.exp(-jnp.abs(logits)))
        la_ref[...] = log_sig * (1.0 / GLA_TAU)
    elif mode == "plain":
        (u_ref,) = rest
        u_ref[...] = u.astype(BF16)
    else:
        rw_ref, u_ref, cmb_ref = rest
        u_ref[...] = u.astype(BF16)
        logits = jnp.dot(u, rw_ref[...], preferred_element_type=F32, precision=lax.Precision.HIGHEST)
        lane = lax.broadcasted_iota(jnp.int32, logits.shape, 1)
        neg = jnp.float32(-jnp.inf)
        logits = jnp.where(lane < n_experts, logits, neg)
        v1 = jnp.max(logits, axis=-1, keepdims=True)
        i1 = jnp.min(jnp.where(logits == v1, lane, LANES), axis=-1, keepdims=True)
        rest_l = jnp.where(lane == i1, neg, logits)
        v2 = jnp.max(rest_l, axis=-1, keepdims=True)
        i2 = jnp.min(jnp.where(rest_l == v2, lane, LANES), axis=-1, keepdims=True)
        e2 = jnp.exp(v2 - v1)
        g1 = 1.0 / (1.0 + e2)
        g2 = e2 / (1.0 + e2)
        cmb_ref[...] = jnp.where(lane == i1, g1, 0.0) + jnp.where(lane == i2, g2, 0.0)


def norm_modulate(h, norm_w, mod, sc_col, sh_col, row_group, tm, *, mode, extra, n_experts=0):
    n, d = h.shape
    in_specs = [
        pl.BlockSpec((tm, d), lambda m: (m, 0)),
        pl.BlockSpec((1, d), lambda m: (0, 0)),
        pl.BlockSpec((None, 1, d), lambda m: (row_group(m, tm), 0, sc_col)),
        pl.BlockSpec((None, 1, d), lambda m: (row_group(m, tm), 0, sh_col)),
    ]
    if mode == "decay":
        wlr, up, bias = extra
        in_specs += [
            pl.BlockSpec(wlr.shape, lambda m: (0, 0)),
            pl.BlockSpec(up.shape, lambda m: (0, 0)),
            pl.BlockSpec(bias.shape, lambda m: (0, 0)),
        ]
        side = up.shape[1]
    elif mode == "router":
        (rw,) = extra
        in_specs += [pl.BlockSpec(rw.shape, lambda m: (0, 0))]
        side = LANES
    out_specs = [pl.BlockSpec((tm, d), lambda m: (m, 0))]
    out_shape = [jax.ShapeDtypeStruct((n, d), BF16)]
    if mode != "plain":
        out_specs += [pl.BlockSpec((tm, side), lambda m: (m, 0))]
        out_shape += [jax.ShapeDtypeStruct((n, side), F32)]
    return pl.pallas_call(
        functools.partial(_norm_kernel, mode=mode, n_experts=n_experts),
        grid=(n // tm,),
        in_specs=in_specs,
        out_specs=out_specs,
        out_shape=out_shape,
        compiler_params=_params("arbitrary"),
        name="norm_" + mode,
    )(h, norm_w.reshape(1, d), mod, mod, *extra)


def _mm_kernel(*refs, mode, nk, nf, gated):
    k = pl.program_id(2)
    if mode == "swiglu":
        x_ref, w1_ref, w3_ref = refs[:3]
        rest = refs[3:]
        if gated:
            cmb_ref, o_ref = rest
        else:
            (o_ref,) = rest
        a = _dot(x_ref[...], w1_ref[...])
        b = _dot(x_ref[...], w3_ref[...])
        hid = _silu(a) * b
        if gated:
            cmb = cmb_ref[...]
            e = pl.program_id(1) // nf
            lane = lax.broadcasted_iota(jnp.int32, cmb.shape, 1)
            hid = hid * jnp.sum(jnp.where(lane == e, cmb, 0.0), axis=-1, keepdims=True)
        o_ref[...] = hid.astype(o_ref.dtype)
        return

    if mode == "resid":
        x_ref, w_ref, h_ref, g_ref, o_ref = refs[:5]
        scratch = refs[5:]
    else:
        x_ref, w_ref, o_ref = refs[:3]
        scratch = refs[3:]

    def finish(acc):
        if mode == "resid":
            o_ref[...] = h_ref[...] + g_ref[...] * acc
        else:
            o_ref[...] = acc.astype(o_ref.dtype)

    part = _dot(x_ref[...], w_ref[...])
    if nk == 1:
        finish(part)
    else:
        (acc_ref,) = scratch

        @pl.when(k == 0)
        def _():
            acc_ref[...] = part

        @pl.when(jnp.logical_and(k > 0, k < nk - 1))
        def _():
            acc_ref[...] += part

        @pl.when(k == nk - 1)
        def _():
            finish(acc_ref[...] + part)


def matmul(x, ws, *, mode, out_dtype, tm, tn, tk, h=None, mod=None, gate_col=None, row_group=None, cmb=None):
    m_tot, k_tot = x.shape
    n_e, _, f = ws[0].shape
    nf = f // tn
    nk = k_tot // tk
    n_tot = n_e * f
    grid = (m_tot // tm, n_tot // tn, nk)
    w_spec = pl.BlockSpec((None, tk, tn), lambda m, n, k: (n // nf, k, n % nf))
    in_specs = [pl.BlockSpec((tm, tk), lambda m, n, k: (m, k))] + [w_spec] * len(ws)
    args = [x, *ws]
    if mode == "resid":
        gate_blk = gate_col // tn
        in_specs += [
            pl.BlockSpec((tm, tn), lambda m, n, k: (m, n)),
            pl.BlockSpec((None, 1, tn), lambda m, n, k: (row_group(m, tm), 0, gate_blk + n)),
        ]
        args += [h, mod]
    if cmb is not None:
        in_specs += [pl.BlockSpec((tm, LANES), lambda m, n, k: (m, 0))]
        args += [cmb]
    scratch = [pltpu.VMEM((tm, tn), F32)] if nk > 1 else []
    return pl.pallas_call(
        functools.partial(_mm_kernel, mode=mode, nk=nk, nf=nf, gated=cmb is not None),
        grid=grid,
        in_specs=in_specs,
        out_specs=pl.BlockSpec((tm, tn), lambda m, n, k: (m, n)),
        out_shape=jax.ShapeDtypeStruct((m_tot, n_tot), out_dtype),
        scratch_shapes=scratch,
        compiler_params=_params("arbitrary", "arbitrary", "arbitrary"),
        name="mm_" + mode,
    )(*args)


def _qkprep_kernel(z_ref, qw_ref, kw_ref, cos_ref, sin_ref, qk_ref, kn_ref, *, n_q, n_kv, hd, scale):
    cos = cos_ref[...]
    sin = sin_ref[...]
    lane = lax.broadcasted_iota(jnp.int32, cos.shape, 1)
    quarter = hd // 4
    first = (lane % (2 * quarter)) < quarter
    for hh in range(n_q + n_kv):
        cols = slice(hh * hd, (hh + 1) * hd)
        x = z_ref[:, cols].astype(F32)
        ms = jnp.mean(x * x, axis=-1, keepdims=True)
        y = x * lax.rsqrt(ms + EPS) * (qw_ref[...] if hh < n_q else kw_ref[...])
        if hh >= n_q:
            kn_ref[:, (hh - n_q) * hd:(hh - n_q + 1) * hd] = y
        partner = jnp.where(first, pltpu.roll(y, hd - quarter, 1), pltpu.roll(y, quarter, 1))
        r = y * cos + partner * sin
        if hh < n_q:
            r = r * scale
        qk_ref[:, cols] = r.astype(BF16)


def qk_prepare(z, q_norm_w, k_norm_w, cos, sin, table_block, tm, *, n_q, n_kv, hd):
    n = z.shape[0]
    w = (n_q + n_kv) * hd
    return pl.pallas_call(
        functools.partial(_qkprep_kernel, n_q=n_q, n_kv=n_kv, hd=hd, scale=hd ** -0.5),
        grid=(n // tm,),
        in_specs=[
            pl.BlockSpec((tm, w), lambda m: (m, 0)),
            pl.BlockSpec((1, hd), lambda m: (0, 0)),
            pl.BlockSpec((1, hd), lambda m: (0, 0)),
            pl.BlockSpec((tm, hd), lambda m: (table_block(m, tm), 0)),
            pl.BlockSpec((tm, hd), lambda m: (table_block(m, tm), 0)),
        ],
        out_specs=[pl.BlockSpec((tm, w), lambda m: (m, 0)), pl.BlockSpec((tm, n_kv * hd), lambda m: (m, 0))],
        out_shape=[jax.ShapeDtypeStruct((n, w), BF16), jax.ShapeDtypeStruct((n, n_kv * hd), F32)],
        compiler_params=_params("arbitrary"),
        name="qk_prep",
    )(z, q_norm_w.reshape(1, hd), k_norm_w.reshape(1, hd), cos, sin)


def _attn_kernel(q_ref, k_ref, v_ref, *rest, group, hd, has_ctx):
    if has_ctx:
        kc_ref, vc_ref, o_ref = rest
        kc = kc_ref[...].astype(BF16)
        vc = vc_ref[...].astype(BF16)
    else:
        (o_ref,) = rest
    k = k_ref[...]
    v = v_ref[...]
    for gi in range(group):
        cols = slice(gi * hd, (gi + 1) * hd)
        q = q_ref[:, cols]
        s = _dot_nt(q, k)
        mx = jnp.max(s, axis=-1, keepdims=True)
        if has_ctx:
            sc = _dot_nt(q, kc)
            mx = jnp.maximum(mx, jnp.max(sc, axis=-1, keepdims=True))
        p = jnp.exp(s - mx)
        den = jnp.sum(p, axis=-1, keepdims=True)
        o = _dot(p.astype(BF16), v)
        if has_ctx:
            pc = jnp.exp(sc - mx)
            den = den + jnp.sum(pc, axis=-1, keepdims=True)
            o = o + _dot(pc.astype(BF16), vc)
        o_ref[:, cols] = (o * (1.0 / den)).astype(o_ref.dtype)


def attention(qk, z, row0, n_batch, seq, *, n_q, n_kv, hd, v_col0, ctx=None):
    group = n_q // n_kv
    tq = _tile(seq, 512)
    nq = seq // tq
    rb_q = row0 // tq
    rb_s = row0 // seq
    in_specs = [
        pl.BlockSpec((tq, group * hd), lambda b, g, i: (rb_q + b * nq + i, g)),
        pl.BlockSpec((seq, hd), lambda b, g, i: (rb_s + b, n_q + g)),
        pl.BlockSpec((seq, hd), lambda b, g, i: (rb_s + b, v_col0 // hd + g)),
    ]
    args = [qk, qk, z]
    if ctx is not None:
        cache_k, cache_v, layer = ctx
        past = cache_k.shape[2]
        c_spec = pl.BlockSpec((None, None, past, hd), lambda b, g, i: (b, layer, 0, g))
        in_specs += [c_spec, c_spec]
        args += [cache_k.reshape(*cache_k.shape[:3], n_kv * hd), cache_v.reshape(*cache_v.shape[:3], n_kv * hd)]
    return pl.pallas_call(
        functools.partial(_attn_kernel, group=group, hd=hd, has_ctx=ctx is not None),
        grid=(n_batch, n_kv, nq),
        in_specs=in_specs,
        out_specs=pl.BlockSpec((tq, group * hd), lambda b, g, i: (b * nq + i, g)),
        out_shape=jax.ShapeDtypeStruct((n_batch * seq, n_q * hd), BF16),
        compiler_params=_params("arbitrary", "arbitrary", "arbitrary"),
        name="attn_ctx" if ctx is not None else "attn",
    )(*args)


def _gla_kernel(q_ref, k_ref, v_ref, g_ref, laf_ref, lab_ref, nw_ref, *rest, seq, chunk, has_s0, want_state, scale):
    rest = list(rest)
    s0_ref = rest.pop(0) if has_s0 else None
    o_ref = rest.pop(0)
    sout_ref = rest.pop(0) if want_state else None
    of_ref, ob_ref, stf_ref, stb_ref = rest
    n = seq // chunk

    row = lax.broadcasted_iota(jnp.int32, (chunk, chunk), 0)
    col = lax.broadcasted_iota(jnp.int32, (chunk, chunk), 1)
    lower = col <= row
    upper = col >= row
    lower_b = lower.astype(BF16)
    upper_b = upper.astype(BF16)

    if has_s0:
        stf_ref[...] = s0_ref[0].T
        stb_ref[...] = s0_ref[1].T
    else:
        stf_ref[...] = jnp.zeros_like(stf_ref)
        stb_ref[...] = jnp.zeros_like(stb_ref)

    def one_chunk(i, tri_b, mask, la_ref, tot_row, st_ref, out_ref):
        sl = pl.ds(pl.multiple_of(i * chunk, chunk), chunk)
        la_hi, la_lo = _split_bf16(la_ref[sl, :])
        cum = _dot(tri_b, la_hi) + _dot(tri_b, la_lo)
        tot = cum[tot_row:tot_row + 1, :]
        q = q_ref[sl, :].astype(F32) * scale
        k = k_ref[sl, :].astype(F32)
        v = v_ref[sl, :]
        q_dec = (q * jnp.exp(cum)).astype(BF16)
        k_inv = (k * jnp.exp(-cum)).astype(BF16)
        k_end = (k * jnp.exp(tot - cum)).astype(BF16)
        att = jnp.where(mask, _dot_nt(q_dec, k_inv), 0.0).astype(BF16)
        st = st_ref[...]
        out_ref[sl, :] = _dot(att, v) + _dot_nt(q_dec, st.astype(BF16))
        st_ref[...] = jnp.exp(tot) * st + _dot_tn(v, k_end)

    def body(j, carry):
        one_chunk(j, lower_b, lower, laf_ref, chunk - 1, stf_ref, of_ref)
        one_chunk(n - 1 - j, upper_b, upper, lab_ref, 0, stb_ref, ob_ref)
        return carry

    lax.fori_loop(0, n, body, 0, unroll=True if n <= 4 else 2)

    if want_state:
        sout_ref[0] = stf_ref[...].T
        sout_ref[1] = stb_ref[...].T

    o = of_ref[...] + ob_ref[...]
    ms = jnp.mean(o * o, axis=-1, keepdims=True)
    y = o * lax.rsqrt(ms + EPS) * nw_ref[...]
    o_ref[...] = (y * _silu(g_ref[...].astype(F32))).astype(o_ref.dtype)


def gla(z, la, norm_w, row0, n_batch, seq, *, heads, dk, dv, q_col0, k_col0, v_col0, g_col0, s0=None,
        want_state=False):
    rb = row0 // seq
    in_specs = [
        pl.BlockSpec((seq, dk), lambda b, hh: (rb + b, q_col0 // dk + hh)),
        pl.BlockSpec((seq, dk), lambda b, hh: (rb + b, k_col0 // dk + hh)),
        pl.BlockSpec((seq, dv), lambda b, hh: (rb + b, v_col0 // dv + hh)),
        pl.BlockSpec((seq, dv), lambda b, hh: (rb + b, g_col0 // dv + hh)),
        pl.BlockSpec((seq, dk), lambda b, hh: (rb + b, hh)),
        pl.BlockSpec((seq, dk), lambda b, hh: (rb + b, heads + hh)),
        pl.BlockSpec((1, dv), lambda b, hh: (0, 0)),
    ]
    args = [z, z, z, z, la, la, norm_w.reshape(1, dv)]
    if s0 is not None:
        state, layer = s0
        in_specs += [pl.BlockSpec((None, None, 2, None, dk, dv), lambda b, hh: (b, layer, 0, hh, 0, 0))]
        args += [state]
    out_specs = [pl.BlockSpec((seq, dv), lambda b, hh: (b, hh))]
    out_shape = [jax.ShapeDtypeStruct((n_batch * seq, heads * dv), BF16)]
    if want_state:
        out_specs += [pl.BlockSpec((None, 2, None, dk, dv), lambda b, hh: (b, 0, hh, 0, 0))]
        out_shape += [jax.ShapeDtypeStruct((n_batch, 2, heads, dk, dv), F32)]
    res = pl.pallas_call(
        functools.partial(_gla_kernel, seq=seq, chunk=GLA_CHUNK, has_s0=s0 is not None, want_state=want_state,
                          scale=dk ** -0.5),
        grid=(n_batch, heads),
        in_specs=in_specs,
        out_specs=out_specs,
        out_shape=out_shape,
        scratch_shapes=[pltpu.VMEM((seq, dv), F32), pltpu.VMEM((seq, dv), F32),
                        pltpu.VMEM((dv, dk), F32), pltpu.VMEM((dv, dk), F32)],
        compiler_params=_params("arbitrary", "arbitrary"),
        name="gla_state" if want_state else "gla",
    )(*args)
    return res if want_state else (res[0], None)


def _rope_tables(seq, hd, lead):
    axis_dim = hd // 2
    t = jnp.arange(seq)
    rowp = (t // GRID_W).astype(F32)
    colp = (t % GRID_W).astype(F32)
    inv = ROPE_THETA ** (-jnp.arange(axis_dim // 2, dtype=F32) * 2.0 / axis_dim)
    ang_r = rowp[:, None] * inv
    ang_c = colp[:, None] * inv
    cos = jnp.concatenate([jnp.cos(ang_r), jnp.cos(ang_r), jnp.cos(ang_c), jnp.cos(ang_c)], axis=-1)
    sin = jnp.concatenate([-jnp.sin(ang_r), jnp.sin(ang_r), -jnp.sin(ang_c), jnp.sin(ang_c)], axis=-1)
    cos = jnp.concatenate([jnp.ones((lead, hd), F32), cos], axis=0)
    sin = jnp.concatenate([jnp.zeros((lead, hd), F32), sin], axis=0)
    return cos, sin


def kernel(x_prompt, x_sample, cache_k, cache_v, state_gla, c, c_ctx, norm1_w, norm2_w, w_ada, b_ada, w_in,
           q_norm_w, k_norm_w, gla_up, gla_bias, gla_norm_w, w_out, ffn_w1, ffn_w3, ffn_w2, router_w, moe_w1,
           moe_w3, moe_w2):
    batch, seq1, d = x_prompt.shape
    dec_batch, seq2, _ = x_sample.shape
    depth = w_in.shape[0]
    hd = q_norm_w.shape[-1]
    n_kv = cache_k.shape[3]
    n_q = d // 2 // hd
    heads, dk, dv = state_gla.shape[3:]
    rank = gla_up.shape[2]
    n_experts = router_w.shape[-1]
    attn_w, kv_w, key_w, gla_w = n_q * hd, n_kv * hd, heads * dk, heads * dv
    main_cols = attn_w + 2 * kv_w + 2 * key_w + 2 * gla_w
    q_col0 = attn_w + 2 * kv_w
    k_col0 = q_col0 + key_w
    v_col0 = k_col0 + key_w
    g_col0 = v_col0 + gla_w
    n1, n2 = batch * seq1, dec_batch * seq2
    n = n1 + n2
    assert hd == LANES and n1 % seq2 == 0 and 2 * rank <= LANES and n_experts <= LANES

    def row_group(m, tm):
        tok = m * tm
        return jnp.where(tok < n1, 0, 1 + (tok - n1) // seq2)

    def table_block(m, tm):
        tok = m * tm
        return jnp.where(tok < n1, 0, 1 + ((tok - n1) % seq2) // tm)

    n_rows = -(-(1 + dec_batch) // SUBLANES) * SUBLANES
    cond = jnp.zeros((n_rows, d), F32).at[0].set(c_ctx).at[1:1 + dec_batch].set(c)
    mod_all = ada_modulation(cond, w_ada, b_ada)

    tm_row = _tile(math.gcd(n1, seq2), 256)
    tm_prep = _tile(math.gcd(n1, seq2), 512)
    tm_mm = _tile(math.gcd(n1, seq2), 1024)
    cos, sin = _rope_tables(seq2, hd, tm_prep)

    h = jnp.concatenate([x_prompt.reshape(n1, d), x_sample.reshape(n2, d)], axis=0)
    new_k, new_v, new_s = [], [], []
    for l in range(depth):
        mod = mod_all[l].reshape(n_rows, 1, N_MOD * d)
        w_in_l = w_in[l]
        w_main = w_in_l[:, :main_cols].astype(BF16)[None]
        w_lr = jnp.zeros((d, LANES), BF16).at[:, :2 * rank].set(w_in_l[:, main_cols:].astype(BF16))
        up = jnp.zeros((LANES, 2 * key_w), F32)
        up = up.at[:rank, :key_w].set(gla_up[l, 0]).at[rank:2 * rank, key_w:].set(gla_up[l, 1])
        bias = gla_bias[l].reshape(1, 2 * key_w)

        u, la = norm_modulate(h, norm1_w[l], mod, 1, 0, row_group, tm_row, mode="decay", extra=(w_lr, up, bias))
        z = matmul(u, [w_main], mode="plain", out_dtype=BF16, tm=tm_mm, tn=_tile(main_cols, 1024), tk=d)

        qk, kn = qk_prepare(z, q_norm_w[l], k_norm_w[l], cos, sin, table_block, tm_prep, n_q=n_q, n_kv=n_kv, hd=hd)
        new_k.append(kn[:n1].reshape(batch, seq1, n_kv, hd))
        new_v.append(z[:n1, attn_w + kv_w:attn_w + 2 * kv_w].astype(F32).reshape(batch, seq1, n_kv, hd))

        attn1 = attention(qk, z, 0, batch, seq1, n_q=n_q, n_kv=n_kv, hd=hd, v_col0=attn_w + kv_w)
        attn2 = attention(qk, z, n1, dec_batch, seq2, n_q=n_q, n_kv=n_kv, hd=hd, v_col0=attn_w + kv_w,
                          ctx=(cache_k, cache_v, l))
        gla_kw = dict(heads=heads, dk=dk, dv=dv, q_col0=q_col0, k_col0=k_col0, v_col0=v_col0, g_col0=g_col0)
        gla1, st = gla(z, la, gla_norm_w[l], 0, batch, seq1, want_state=True, **gla_kw)
        gla2, _ = gla(z, la, gla_norm_w[l], n1, dec_batch, seq2, s0=(state_gla, l), **gla_kw)
        new_s.append(st)
        mix = jnp.concatenate([jnp.concatenate([attn1, gla1], axis=1), jnp.concatenate([attn2, gla2], axis=1)], axis=0)

        h = matmul(mix, [w_out[l].astype(BF16)[None]], mode="resid", out_dtype=F32, tm=tm_mm, tn=_tile(d, 1024), tk=d,
                   h=h, mod=mod, gate_col=2 * d, row_group=row_group)

        i = l // 2
        if l % 2 == 0:
            (u2,) = norm_modulate(h, norm2_w[l], mod, 4, 3, row_group, tm_row, mode="plain", extra=())
            w1, w3 = ffn_w1[i].astype(BF16)[None], ffn_w3[i].astype(BF16)[None]
            w2 = ffn_w2[i].astype(BF16)[None]
            hid = matmul(u2, [w1, w3], mode="swiglu", out_dtype=BF16, tm=tm_mm, tn=_tile(w1.shape[-1], 512), tk=d)
        else:
            rw = jnp.zeros((d, LANES), F32).at[:, :n_experts].set(router_w[i])
            u2, cmb = norm_modulate(h, norm2_w[l], mod, 4, 3, row_group, tm_row, mode="router", extra=(rw,),
                                    n_experts=n_experts)
            w1, w3 = moe_w1[i].astype(BF16), moe_w3[i].astype(BF16)
            w2 = moe_w2[i].astype(BF16).reshape(1, -1, d)
            hid = matmul(u2, [w1, w3], mode="swiglu", out_dtype=BF16, tm=tm_mm, tn=_tile(w1.shape[-1], 512), tk=d,
                         cmb=cmb)
        h = matmul(hid, [w2], mode="resid", out_dtype=F32, tm=tm_mm, tn=_tile(d, 1024), tk=_tile(hid.shape[1], 2048),
                   h=h, mod=mod, gate_col=5 * d, row_group=row_group)

    y_prompt = h[:n1].reshape(batch, seq1, d)
    y_sample = h[n1:].reshape(dec_batch, seq2, d)
    return (y_prompt, y_sample, jnp.stack(new_k, axis=1), jnp.stack(new_v, axis=1),
            jnp.stack(new_s, axis=1).astype(x_prompt.dtype))
```

```python
import functools
import math

import jax
import jax.numpy as jnp
from jax import lax
from jax.experimental import pallas as pl
from jax.experimental.pallas import tpu as pltpu

F32 = jnp.float32
BF16 = jnp.bfloat16
U32 = jnp.uint32

EPS = 1e-6
GRID_W = 64
GLA_CHUNK = 64
GLA_TAU = 16.0
ROPE_THETA = 10000.0
N_MOD = 6
LANES = 128
SUBLANES = 8
VMEM_LIMIT_BYTES = 56 * 1024 * 1024


def _params(*semantics):
    return pltpu.CompilerParams(dimension_semantics=semantics, vmem_limit_bytes=VMEM_LIMIT_BYTES)


def _tile(n, cap):
    t = cap
    while t > 1 and n % t:
        t //= 2
    return t


def _dot(a, b):
    return jnp.dot(a, b, preferred_element_type=F32)


def _dot_nt(a, b):
    return lax.dot_general(a, b, (((1,), (1,)), ((), ())), preferred_element_type=F32)


def _dot_tn(a, b):
    return lax.dot_general(a, b, (((0,), (0,)), ((), ())), preferred_element_type=F32)


def _split_bf16(x):
    hi = x.astype(BF16)
    lo = (x - hi.astype(F32)).astype(BF16)
    return hi, lo


def _silu(x):
    return x * jax.nn.sigmoid(x)


HI_MASK = 0xFFFF0000


def _pack_halves(x):
    w = x.shape[1] // 2
    bits = lax.bitcast_convert_type(x.astype(BF16).astype(F32), U32)
    return (bits[:, :w] >> 16) | (bits[:, w:] & jnp.uint32(HI_MASK))


def _unpack_halves(p):
    lo = lax.bitcast_convert_type(p << 16, F32)
    hi = lax.bitcast_convert_type(p & jnp.uint32(HI_MASK), F32)
    return lo, hi


def _ada_kernel(c_ref, w_ref, b_ref, o_ref):
    s = _silu(c_ref[...]).astype(BF16)
    o_ref[...] = _dot(s, w_ref[...].astype(BF16)) + b_ref[...]


def ada_modulation(cond, w_ada, b_ada):
    depth, d, n = w_ada.shape
    r = cond.shape[0]
    tn = _tile(n, 1024)
    return pl.pallas_call(
        _ada_kernel,
        grid=(depth, n // tn),
        in_specs=[
            pl.BlockSpec((r, d), lambda l, j: (0, 0)),
            pl.BlockSpec((None, d, tn), lambda l, j: (l, 0, j)),
            pl.BlockSpec((None, 1, tn), lambda l, j: (l, 0, j)),
        ],
        out_specs=pl.BlockSpec((None, r, tn), lambda l, j: (l, 0, j)),
        out_shape=jax.ShapeDtypeStruct((depth, r, n), F32),
        compiler_params=_params("arbitrary", "arbitrary"),
        name="ada",
    )(cond, w_ada, b_ada.reshape(depth, 1, n))


def _norm_kernel(h_ref, nw_ref, sc_ref, sh_ref, *rest, mode, n_experts):
    x = h_ref[...]
    ms = jnp.mean(x * x, axis=-1, keepdims=True)
    y = x * lax.rsqrt(ms + EPS) * nw_ref[...]
    u = y * (1.0 + sc_ref[...]) + sh_ref[...]
    if mode == "decay":
        wlr_ref, up_ref, bias_ref, u_ref, la_ref = rest
        ub = u.astype(BF16)
        u_ref[...] = ub
        lr = _dot(ub, wlr_ref[...])
        lr_hi, lr_lo = _split_bf16(lr)
        up_hi, up_lo = _split_bf16(up_ref[...])
        logits = _dot(lr_hi, up_hi) + (_dot(lr_lo, up_hi) + _dot(lr_hi, up_lo)) + bias_ref[...]
        log_sig = jnp.minimum(logits, 0.0) - jnp.log1p(jnp.exp(-jnp.abs(logits)))
        la = log_sig * (1.0 / GLA_TAU)
        tm = la.shape[0]
        half = la.shape[1] // 2
        t_row = lax.broadcasted_iota(jnp.int32, (tm, tm), 0)
        t_col = lax.broadcasted_iota(jnp.int32, (tm, tm), 1)
        same_chunk = (t_row // GLA_CHUNK) == (t_col // GLA_CHUNK)
        tri_f = jnp.logical_and(same_chunk, t_col <= t_row).astype(BF16)
        tri_b = jnp.logical_and(same_chunk, t_col >= t_row).astype(BF16)
        la_hi, la_lo = _split_bf16(la)
        la_ref[:, :half] = _dot(tri_f, la_hi[:, :half]) + _dot(tri_f, la_lo[:, :half])
        la_ref[:, half:] = _dot(tri_b, la_hi[:, half:]) + _dot(tri_b, la_lo[:, half:])
    elif mode == "plain":
        (u_ref,) = rest
        u_ref[...] = u.astype(BF16)
    else:
        rw_ref, xp_ref, meta_ref, cnt_ref, run_ref = rest
        xp_ref[...] = _pack_halves(u)
        logits = jnp.dot(u, rw_ref[...], preferred_element_type=F32, precision=lax.Precision.HIGHEST)
        lane = lax.broadcasted_iota(jnp.int32, logits.shape, 1)
        neg = jnp.float32(-jnp.inf)
        logits = jnp.where(lane < n_experts, logits, neg)
        v1 = jnp.max(logits, axis=-1, keepdims=True)
        i1 = jnp.min(jnp.where(logits == v1, lane, LANES), axis=-1, keepdims=True)
        rest_l = jnp.where(lane == i1, neg, logits)
        v2 = jnp.max(rest_l, axis=-1, keepdims=True)
        i2 = jnp.min(jnp.where(rest_l == v2, lane, LANES), axis=-1, keepdims=True)
        e2 = jnp.exp(v2 - v1)
        g1 = 1.0 / (1.0 + e2)
        g2 = e2 / (1.0 + e2)

        @pl.when(pl.program_id(0) == 0)
        def _():
            run_ref[...] = jnp.zeros_like(run_ref)

        tm = logits.shape[0]
        chosen = jnp.logical_or(lane == i1, lane == i2)
        t_row = lax.broadcasted_iota(jnp.int32, (tm, tm), 0)
        t_col = lax.broadcasted_iota(jnp.int32, (tm, tm), 1)
        before = _dot((t_col < t_row).astype(BF16), chosen.astype(BF16)) + run_ref[...]
        r1 = jnp.sum(jnp.where(lane == i1, before, 0.0), axis=-1, keepdims=True)
        r2 = jnp.sum(jnp.where(lane == i2, before, 0.0), axis=-1, keepdims=True)
        run_ref[...] += jnp.sum(chosen.astype(F32), axis=0, keepdims=True)
        cnt_ref[...] = jnp.broadcast_to(run_ref[...], cnt_ref.shape)
        fields = (i1.astype(F32), i2.astype(F32), g1, g2, r1, r2)
        meta = jnp.zeros(logits.shape, F32)
        for pos, val in enumerate(fields):
            meta = jnp.where(lane == pos, val, meta)
        meta_ref[...] = meta


def norm_modulate(h, norm_w, mod, sc_col, sh_col, row_group, tm, *, mode, extra, n_experts=0):
    n, d = h.shape
    in_specs = [
        pl.BlockSpec((tm, d), lambda m: (m, 0)),
        pl.BlockSpec((1, d), lambda m: (0, 0)),
        pl.BlockSpec((None, 1, d), lambda m: (row_group(m, tm), 0, sc_col)),
        pl.BlockSpec((None, 1, d), lambda m: (row_group(m, tm), 0, sh_col)),
    ]
    if mode == "decay":
        wlr, up, bias = extra
        in_specs += [
            pl.BlockSpec(wlr.shape, lambda m: (0, 0)),
            pl.BlockSpec(up.shape, lambda m: (0, 0)),
            pl.BlockSpec(bias.shape, lambda m: (0, 0)),
        ]
        side = up.shape[1]
    elif mode == "router":
        (rw,) = extra
        in_specs += [pl.BlockSpec(rw.shape, lambda m: (0, 0))]
    scratch = []
    if mode == "router":
        out_specs = [pl.BlockSpec((tm, d // 2), lambda m: (m, 0)), pl.BlockSpec((tm, LANES), lambda m: (m, 0)),
                     pl.BlockSpec((SUBLANES, LANES), lambda m: (0, 0))]
        out_shape = [jax.ShapeDtypeStruct((n, d // 2), U32), jax.ShapeDtypeStruct((n, LANES), F32),
                     jax.ShapeDtypeStruct((SUBLANES, LANES), F32)]
        scratch = [pltpu.VMEM((1, LANES), F32)]
    else:
        out_specs = [pl.BlockSpec((tm, d), lambda m: (m, 0))]
        out_shape = [jax.ShapeDtypeStruct((n, d), BF16)]
        if mode == "decay":
            out_specs += [pl.BlockSpec((tm, side), lambda m: (m, 0))]
            out_shape += [jax.ShapeDtypeStruct((n, side), F32)]
    return pl.pallas_call(
        functools.partial(_norm_kernel, mode=mode, n_experts=n_experts),
        grid=(n // tm,),
        in_specs=in_specs,
        out_specs=out_specs,
        out_shape=out_shape,
        scratch_shapes=scratch,
        compiler_params=_params("arbitrary"),
        name="norm_" + mode,
    )(h, norm_w.reshape(1, d), mod, mod, *extra)


def _mm_kernel(*refs, mode, nk):
    k = pl.program_id(2)
    if mode == "swiglu":
        x_ref, w1_ref, w3_ref, o_ref = refs
        a = _dot(x_ref[...], w1_ref[...])
        b = _dot(x_ref[...], w3_ref[...])
        o_ref[...] = (_silu(a) * b).astype(o_ref.dtype)
        return

    if mode == "resid":
        x_ref, w_ref, h_ref, g_ref, o_ref = refs[:5]
        scratch = refs[5:]
    else:
        x_ref, w_ref, o_ref = refs[:3]
        scratch = refs[3:]

    def finish(acc):
        if mode == "resid":
            o_ref[...] = h_ref[...] + g_ref[...] * acc
        else:
            o_ref[...] = acc.astype(o_ref.dtype)

    part = _dot(x_ref[...], w_ref[...])
    if nk == 1:
        finish(part)
    else:
        (acc_ref,) = scratch

        @pl.when(k == 0)
        def _():
            acc_ref[...] = part

        @pl.when(jnp.logical_and(k > 0, k < nk - 1))
        def _():
            acc_ref[...] += part

        @pl.when(k == nk - 1)
        def _():
            finish(acc_ref[...] + part)


def matmul(x, ws, *, mode, out_dtype, tm, tn, tk, h=None, mod=None, gate_col=None, row_group=None):
    m_tot, k_tot = x.shape
    n_tot = ws[0].shape[1]
    nk = k_tot // tk
    grid = (m_tot // tm, n_tot // tn, nk)
    w_spec = pl.BlockSpec((tk, tn), lambda m, n, k: (k, n))
    in_specs = [pl.BlockSpec((tm, tk), lambda m, n, k: (m, k))] + [w_spec] * len(ws)
    args = [x, *ws]
    if mode == "resid":
        gate_blk = gate_col // tn
        in_specs += [
            pl.BlockSpec((tm, tn), lambda m, n, k: (m, n)),
            pl.BlockSpec((None, 1, tn), lambda m, n, k: (row_group(m, tm), 0, gate_blk + n)),
        ]
        args += [h, mod]
    scratch = [pltpu.VMEM((tm, tn), F32)] if nk > 1 else []
    return pl.pallas_call(
        functools.partial(_mm_kernel, mode=mode, nk=nk),
        grid=grid,
        in_specs=in_specs,
        out_specs=pl.BlockSpec((tm, tn), lambda m, n, k: (m, n)),
        out_shape=jax.ShapeDtypeStruct((m_tot, n_tot), out_dtype),
        scratch_shapes=scratch,
        compiler_params=_params("arbitrary", "arbitrary", "arbitrary"),
        name="mm_" + mode,
    )(*args)


def _rowcopy_kernel(src_ref, dst_ref, x_hbm, *rest, rows):
    o_hbm, sem = rest[-2:]

    def issue(r, carry):
        pltpu.make_async_copy(x_hbm.at[pl.ds(src_ref[0, r], 1)], o_hbm.at[pl.ds(dst_ref[0, r], 1)], sem).start()
        return carry

    lax.fori_loop(0, rows, issue, 0)

    def drain(r, carry):
        pltpu.make_async_copy(x_hbm.at[pl.ds(0, 1)], o_hbm.at[pl.ds(0, 1)], sem).wait()
        return carry

    lax.fori_loop(0, rows, drain, 0)


def row_copy(x, src, dst, n_out, init=None):
    total = src.shape[0]
    rows = _tile(total, 2048)
    idx_spec = pl.BlockSpec((None, 1, rows), lambda g: (g, 0, 0), memory_space=pltpu.SMEM)
    any_spec = pl.BlockSpec(memory_space=pl.ANY)
    args = [src.reshape(total // rows, 1, rows), dst.reshape(total // rows, 1, rows), x]
    in_specs = [idx_spec, idx_spec, any_spec]
    aliases = {}
    if init is not None:
        args.append(init)
        in_specs.append(any_spec)
        aliases = {3: 0}
    return pl.pallas_call(
        functools.partial(_rowcopy_kernel, rows=rows),
        grid=(total // rows,),
        in_specs=in_specs,
        out_specs=any_spec,
        out_shape=jax.ShapeDtypeStruct((n_out, x.shape[1]), x.dtype),
        scratch_shapes=[pltpu.SemaphoreType.DMA(())],
        input_output_aliases=aliases,
        compiler_params=_params("arbitrary"),
        name="row_copy",
    )(*args)


def _expert_swiglu_kernel(te_ref, na_ref, xs_ref, w1_ref, w3_ref, o_ref, lo_ref, hi_ref):
    i = pl.program_id(0)
    active = i < na_ref[0]
    half = xs_ref.shape[1]

    @pl.when(jnp.logical_and(active, pl.program_id(1) == 0))
    def _():
        lo, hi = _unpack_halves(xs_ref[...])
        lo_ref[...] = lo.astype(BF16)
        hi_ref[...] = hi.astype(BF16)

    @pl.when(active)
    def _():
        a = _dot(lo_ref[...], w1_ref[:half, :]) + _dot(hi_ref[...], w1_ref[half:, :])
        b = _dot(lo_ref[...], w3_ref[:half, :]) + _dot(hi_ref[...], w3_ref[half:, :])
        o_ref[...] = (_silu(a) * b).astype(o_ref.dtype)

    @pl.when(jnp.logical_not(active))
    def _():
        o_ref[...] = jnp.zeros_like(o_ref)


def _expert_down_kernel(te_ref, na_ref, x_ref, w_ref, o_ref):
    active = pl.program_id(0) < na_ref[0]

    @pl.when(active)
    def _():
        o_ref[...] = _pack_halves(_dot(x_ref[...], w_ref[...]))

    @pl.when(jnp.logical_not(active))
    def _():
        o_ref[...] = jnp.zeros_like(o_ref)


def expert_ffn(xs, w1, w3, w2, tile_expert, n_active, tm, pack_tn):
    p_rows, half = xs.shape
    n_e, d, f = w1.shape
    tn = _tile(f, 512)
    nf = f // tn

    def w13_map(i, j, te, na):
        return te[i], 0, jnp.where(i < na[0], j, nf - 1)

    hid = pl.pallas_call(
        _expert_swiglu_kernel,
        grid_spec=pltpu.PrefetchScalarGridSpec(
            num_scalar_prefetch=2,
            grid=(p_rows // tm, nf),
            in_specs=[
                pl.BlockSpec((tm, half), lambda i, j, te, na: (i, 0)),
                pl.BlockSpec((None, d, tn), w13_map),
                pl.BlockSpec((None, d, tn), w13_map),
            ],
            out_specs=pl.BlockSpec((tm, tn), lambda i, j, te, na: (i, j)),
            scratch_shapes=[pltpu.VMEM((tm, half), BF16), pltpu.VMEM((tm, half), BF16)],
        ),
        out_shape=jax.ShapeDtypeStruct((p_rows, f), BF16),
        compiler_params=_params("arbitrary", "arbitrary"),
        name="expert_swiglu",
    )(tile_expert, n_active, xs, w1, w3)

    nd = d // pack_tn

    def w2_map(i, j, te, na):
        return te[i], 0, jnp.where(i < na[0], j, nd - 1)

    return pl.pallas_call(
        _expert_down_kernel,
        grid_spec=pltpu.PrefetchScalarGridSpec(
            num_scalar_prefetch=2,
            grid=(p_rows // tm, nd),
            in_specs=[
                pl.BlockSpec((tm, f), lambda i, j, te, na: (i, 0)),
                pl.BlockSpec((None, f, pack_tn), w2_map),
            ],
            out_specs=pl.BlockSpec((tm, pack_tn // 2), lambda i, j, te, na: (i, j)),
        ),
        out_shape=jax.ShapeDtypeStruct((p_rows, d // 2), U32),
        compiler_params=_params("arbitrary", "arbitrary"),
        name="expert_down",
    )(tile_expert, n_active, hid, w2)


def _combine_kernel(a_ref, b_ref, meta_ref, h_ref, g_ref, o_ref):
    meta = meta_ref[...]
    lane = lax.broadcasted_iota(jnp.int32, meta.shape, 1)
    g1 = jnp.sum(jnp.where(lane == 2, meta, 0.0), axis=-1, keepdims=True)
    g2 = jnp.sum(jnp.where(lane == 3, meta, 0.0), axis=-1, keepdims=True)
    a_lo, a_hi = _unpack_halves(a_ref[...])
    b_lo, b_hi = _unpack_halves(b_ref[...])
    half = a_lo.shape[1]
    gate = g_ref[...]
    o_ref[:, :half] = h_ref[:, :half] + gate[:, :half] * (g1 * a_lo + g2 * b_lo)
    o_ref[:, half:] = h_ref[:, half:] + gate[:, half:] * (g1 * a_hi + g2 * b_hi)


def combine_experts(yg, meta, h, mod, gate_col, row_group, tm, pack_tn):
    n, d = h.shape
    nd = d // pack_tn
    gate_blk = gate_col // pack_tn
    return pl.pallas_call(
        _combine_kernel,
        grid=(n // tm, nd),
        in_specs=[
            pl.BlockSpec((tm, pack_tn // 2), lambda m, j: (m, j)),
            pl.BlockSpec((tm, pack_tn // 2), lambda m, j: (m, nd + j)),
            pl.BlockSpec((tm, LANES), lambda m, j: (m, 0)),
            pl.BlockSpec((tm, pack_tn), lambda m, j: (m, j)),
            pl.BlockSpec((None, 1, pack_tn), lambda m, j: (row_group(m, tm), 0, gate_blk + j)),
        ],
        out_specs=pl.BlockSpec((tm, pack_tn), lambda m, j: (m, j)),
        out_shape=jax.ShapeDtypeStruct((n, d), F32),
        compiler_params=_params("arbitrary", "arbitrary"),
        name="combine",
    )(yg, yg, meta, h, mod)


def moe_ffn(xp, meta, counts, h, mod, gate_col, row_group, w1, w3, w2, tm_e, tm_c):
    n, d = h.shape
    n_e = w1.shape[0]
    pack_tn = _tile(d, 1024)
    experts = meta[:, 0:2].astype(jnp.int32)
    ranks = meta[:, 4:6].astype(jnp.int32)
    cnt = counts[0, :n_e].astype(jnp.int32)
    seg = (cnt + tm_e - 1) // tm_e * tm_e
    seg_end = jnp.cumsum(seg)
    seg_start = seg_end - seg
    onehot = experts[:, :, None] == jnp.arange(n_e)[None, None, :]
    pos = jnp.sum(jnp.where(onehot, seg_start[None, None, :], 0), axis=-1) + ranks
    p_rows = 2 * n + n_e * tm_e
    n_tiles = p_rows // tm_e
    n_active = (seg_end[-1] // tm_e).reshape(1)
    tile_row = jnp.arange(n_tiles) * tm_e
    tile_expert = jnp.sum(tile_row[:, None] >= seg_end[None, :], axis=1)
    last = jnp.sum(tile_expert * (jnp.arange(n_tiles) == n_active[0] - 1))
    tile_expert = jnp.where(jnp.arange(n_tiles) < n_active[0], tile_expert, last).astype(jnp.int32)

    tok2 = jnp.repeat(jnp.arange(n, dtype=jnp.int32), 2)
    flat = pos.reshape(-1).astype(jnp.int32)
    xs = row_copy(xp, tok2, flat, p_rows, init=jnp.zeros((p_rows, d // 2), U32))
    ys = expert_ffn(xs, w1, w3, w2, tile_expert, n_active.astype(jnp.int32), tm_e, pack_tn)
    yg = row_copy(ys, flat, jnp.arange(2 * n, dtype=jnp.int32), 2 * n).reshape(n, d)
    return combine_experts(yg, meta, h, mod, gate_col, row_group, tm_c, pack_tn)


def _qkprep_kernel(z_ref, qw_ref, kw_ref, cos_ref, sin_ref, qk_ref, kn_ref, *, n_q, n_kv, hd, scale):
    cos = cos_ref[...]
    sin = sin_ref[...]
    lane = lax.broadcasted_iota(jnp.int32, cos.shape, 1)
    quarter = hd // 4
    first = (lane % (2 * quarter)) < quarter
    for hh in range(n_q + n_kv):
        cols = slice(hh * hd, (hh + 1) * hd)
        x = z_ref[:, cols].astype(F32)
        ms = jnp.mean(x * x, axis=-1, keepdims=True)
        y = x * lax.rsqrt(ms + EPS) * (qw_ref[...] if hh < n_q else kw_ref[...])
        if hh >= n_q:
            kn_ref[:, (hh - n_q) * hd:(hh - n_q + 1) * hd] = y
        partner = jnp.where(first, pltpu.roll(y, hd - quarter, 1), pltpu.roll(y, quarter, 1))
        r = y * cos + partner * sin
        if hh < n_q:
            r = r * scale
        qk_ref[:, cols] = r.astype(BF16)


def qk_prepare(z, q_norm_w, k_norm_w, cos, sin, table_block, tm, *, n_q, n_kv, hd):
    n = z.shape[0]
    w = (n_q + n_kv) * hd
    return pl.pallas_call(
        functools.partial(_qkprep_kernel, n_q=n_q, n_kv=n_kv, hd=hd, scale=hd ** -0.5),
        grid=(n // tm,),
        in_specs=[
            pl.BlockSpec((tm, w), lambda m: (m, 0)),
            pl.BlockSpec((1, hd), lambda m: (0, 0)),
            pl.BlockSpec((1, hd), lambda m: (0, 0)),
            pl.BlockSpec((tm, hd), lambda m: (table_block(m, tm), 0)),
            pl.BlockSpec((tm, hd), lambda m: (table_block(m, tm), 0)),
        ],
        out_specs=[pl.BlockSpec((tm, w), lambda m: (m, 0)), pl.BlockSpec((tm, n_kv * hd), lambda m: (m, 0))],
        out_shape=[jax.ShapeDtypeStruct((n, w), BF16), jax.ShapeDtypeStruct((n, n_kv * hd), F32)],
        compiler_params=_params("arbitrary"),
        name="qk_prep",
    )(z, q_norm_w.reshape(1, hd), k_norm_w.reshape(1, hd), cos, sin)


def _attn_kernel(q_ref, k_ref, v_ref, *rest, group, hd, has_ctx):
    o_ref = rest[-1]
    if has_ctx:
        kc_ref, vc_ref = rest[:2]
        kc = kc_ref[...].astype(BF16)
        vc = vc_ref[...].astype(BF16)
    k = k_ref[...]
    v = v_ref[...]
    for gi in range(group):
        cols = slice(gi * hd, (gi + 1) * hd)
        q = q_ref[:, cols]
        s = _dot_nt(q, k)
        mx = jnp.max(s, axis=-1, keepdims=True)
        if has_ctx:
            sc = _dot_nt(q, kc)
            mx = jnp.maximum(mx, jnp.max(sc, axis=-1, keepdims=True))
        p = jnp.exp(s - mx)
        den = jnp.sum(p, axis=-1, keepdims=True)
        o = _dot(p.astype(BF16), v)
        if has_ctx:
            pc = jnp.exp(sc - mx)
            den = den + jnp.sum(pc, axis=-1, keepdims=True)
            o = o + _dot(pc.astype(BF16), vc)
        o_ref[:, cols] = (o * (1.0 / den)).astype(o_ref.dtype)


def _mix_target(mix, mix_shape, args, in_specs):
    if mix is None:
        return jax.ShapeDtypeStruct(mix_shape, BF16), {}
    args.append(mix)
    in_specs.append(pl.BlockSpec(memory_space=pl.ANY))
    return jax.ShapeDtypeStruct(mix.shape, mix.dtype), {len(args) - 1: 0}


def attention(qk, z, row0, n_batch, seq, *, n_q, n_kv, hd, v_col0, mix, mix_shape, ctx=None):
    group = n_q // n_kv
    tq = _tile(seq, 512)
    nq = seq // tq
    rb_q = row0 // tq
    rb_s = row0 // seq
    in_specs = [
        pl.BlockSpec((tq, group * hd), lambda b, g, i: (rb_q + b * nq + i, g)),
        pl.BlockSpec((seq, hd), lambda b, g, i: (rb_s + b, n_q + g)),
        pl.BlockSpec((seq, hd), lambda b, g, i: (rb_s + b, v_col0 // hd + g)),
    ]
    args = [qk, qk, z]
    if ctx is not None:
        cache_k, cache_v, layer = ctx
        past = cache_k.shape[2]
        c_spec = pl.BlockSpec((None, None, past, hd), lambda b, g, i: (b, layer, 0, g))
        in_specs += [c_spec, c_spec]
        args += [cache_k.reshape(*cache_k.shape[:3], n_kv * hd), cache_v.reshape(*cache_v.shape[:3], n_kv * hd)]
    out_shape, aliases = _mix_target(mix, mix_shape, args, in_specs)
    return pl.pallas_call(
        functools.partial(_attn_kernel, group=group, hd=hd, has_ctx=ctx is not None),
        grid=(n_batch, n_kv, nq),
        in_specs=in_specs,
        out_specs=pl.BlockSpec((tq, group * hd), lambda b, g, i: (rb_q + b * nq + i, g)),
        out_shape=out_shape,
        input_output_aliases=aliases,
        compiler_params=_params("arbitrary", "arbitrary", "arbitrary"),
        name="attn_ctx" if ctx is not None else "attn",
    )(*args)


def _gla_kernel(q_ref, k_ref, v_ref, g_ref, cf_ref, cb_ref, nw_ref, *rest, seq, chunk, has_s0, want_state, scale,
                n_aliased):
    rest = list(rest)
    s0_ref = rest.pop(0) if has_s0 else None
    del rest[:n_aliased]
    o_ref = rest.pop(0)
    sout_ref = rest.pop(0) if want_state else None
    qdf_ref, qdb_ref, oin_ref, kvf_ref, kvb_ref, spf_ref, spb_ref = rest
    n = seq // chunk
    unroll = True if n <= 4 else 4

    row = lax.broadcasted_iota(jnp.int32, (chunk, chunk), 0)
    col = lax.broadcasted_iota(jnp.int32, (chunk, chunk), 1)
    lower = col <= row
    upper = col >= row

    def rows(i):
        return pl.ds(pl.multiple_of(i * chunk, chunk), chunk)

    def intra(i, cum_ref, mask, tot_row, qd_ref, kv_ref):
        sl = rows(i)
        cum = cum_ref[sl, :]
        tot = cum[tot_row:tot_row + 1, :]
        q = q_ref[sl, :].astype(F32) * scale
        k = k_ref[sl, :].astype(F32)
        v = v_ref[sl, :]
        q_dec = (q * jnp.exp(cum)).astype(BF16)
        k_inv = (k * jnp.exp(-cum)).astype(BF16)
        k_end = (k * jnp.exp(tot - cum)).astype(BF16)
        qd_ref[sl, :] = q_dec
        kv_ref[i] = _dot_tn(v, k_end)
        att = jnp.where(mask, _dot_nt(q_dec, k_inv), 0.0).astype(BF16)
        return _dot(att, v)

    def phase1(i, carry):
        oin_ref[rows(i), :] = (intra(i, cf_ref, lower, chunk - 1, qdf_ref, kvf_ref)
                               + intra(i, cb_ref, upper, 0, qdb_ref, kvb_ref))
        return carry

    lax.fori_loop(0, n, phase1, 0, unroll=unroll)

    def scan(reverse, cum_ref, tot_row, kv_ref, sp_ref, init):
        def step(j, st):
            i = n - 1 - j if reverse else j
            tot = cum_ref[pl.ds(i * chunk + tot_row, 1), :]
            sp_ref[i] = st.astype(BF16)
            return jnp.exp(tot) * st + kv_ref[i]

        return lax.fori_loop(0, n, step, init)

    zero = jnp.zeros(kvf_ref.shape[1:], F32)
    st_f = scan(False, cf_ref, chunk - 1, kvf_ref, spf_ref, s0_ref[0].T if has_s0 else zero)
    st_b = scan(True, cb_ref, 0, kvb_ref, spb_ref, s0_ref[1].T if has_s0 else zero)
    if want_state:
        sout_ref[0] = st_f.T
        sout_ref[1] = st_b.T

    def phase3(i, carry):
        sl = rows(i)
        o = oin_ref[sl, :] + _dot_nt(qdf_ref[sl, :], spf_ref[i]) + _dot_nt(qdb_ref[sl, :], spb_ref[i])
        ms = jnp.mean(o * o, axis=-1, keepdims=True)
        y = o * lax.rsqrt(ms + EPS) * nw_ref[...]
        o_ref[sl, :] = (y * _silu(g_ref[sl, :].astype(F32))).astype(o_ref.dtype)
        return carry

    lax.fori_loop(0, n, phase3, 0, unroll=unroll)


def gla(z, la, norm_w, row0, n_batch, seq, mix, mix_col0, *, heads, dk, dv, q_col0, k_col0, v_col0, g_col0, s0=None,
        new_state=None):
    rb = row0 // seq
    n_chunks = seq // GLA_CHUNK
    want_state = new_state is not None
    in_specs = [
        pl.BlockSpec((seq, dk), lambda b, hh: (rb + b, q_col0 // dk + hh)),
        pl.BlockSpec((seq, dk), lambda b, hh: (rb + b, k_col0 // dk + hh)),
        pl.BlockSpec((seq, dv), lambda b, hh: (rb + b, v_col0 // dv + hh)),
        pl.BlockSpec((seq, dv), lambda b, hh: (rb + b, g_col0 // dv + hh)),
        pl.BlockSpec((seq, dk), lambda b, hh: (rb + b, hh)),
        pl.BlockSpec((seq, dk), lambda b, hh: (rb + b, heads + hh)),
        pl.BlockSpec((1, dv), lambda b, hh: (0, 0)),
    ]
    args = [z, z, z, z, la, la, norm_w.reshape(1, dv)]
    if s0 is not None:
        state, layer = s0
        in_specs += [pl.BlockSpec((None, None, 2, None, dk, dv), lambda b, hh: (b, layer, 0, hh, 0, 0))]
        args += [state]
    n_fixed = len(args)
    mix_shape, aliases = _mix_target(mix, None, args, in_specs)
    out_specs = [pl.BlockSpec((seq, dv), lambda b, hh: (rb + b, mix_col0 // dv + hh))]
    out_shape = [mix_shape]
    if want_state:
        states, layer_out, depth = new_state
        out_specs += [pl.BlockSpec((None, None, 2, None, dk, dv), lambda b, hh: (b, layer_out, 0, hh, 0, 0))]
        out_shape += [jax.ShapeDtypeStruct((n_batch, depth, 2, heads, dk, dv), F32)]
        if states is not None:
            args.append(states)
            in_specs.append(pl.BlockSpec(memory_space=pl.ANY))
            aliases[len(args) - 1] = 1
    res = pl.pallas_call(
        functools.partial(_gla_kernel, seq=seq, chunk=GLA_CHUNK, has_s0=s0 is not None, want_state=want_state,
                          scale=dk ** -0.5, n_aliased=len(args) - n_fixed),
        grid=(n_batch, heads),
        in_specs=in_specs,
        out_specs=out_specs,
        out_shape=out_shape,
        input_output_aliases=aliases,
        scratch_shapes=[pltpu.VMEM((seq, dk), BF16), pltpu.VMEM((seq, dk), BF16), pltpu.VMEM((seq, dv), F32),
                        pltpu.VMEM((n_chunks, dv, dk), F32), pltpu.VMEM((n_chunks, dv, dk), F32),
                        pltpu.VMEM((n_chunks, dv, dk), BF16), pltpu.VMEM((n_chunks, dv, dk), BF16)],
        compiler_params=_params("arbitrary", "arbitrary"),
        name="gla_state" if want_state else "gla",
    )(*args)
    return res if want_state else (res[0], None)


def _rope_tables(seq, hd, lead):
    axis_dim = hd // 2
    t = jnp.arange(seq)
    rowp = (t // GRID_W).astype(F32)
    colp = (t % GRID_W).astype(F32)
    inv = ROPE_THETA ** (-jnp.arange(axis_dim // 2, dtype=F32) * 2.0 / axis_dim)
    ang_r = rowp[:, None] * inv
    ang_c = colp[:, None] * inv
    cos = jnp.concatenate([jnp.cos(ang_r), jnp.cos(ang_r), jnp.cos(ang_c), jnp.cos(ang_c)], axis=-1)
    sin = jnp.concatenate([-jnp.sin(ang_r), jnp.sin(ang_r), -jnp.sin(ang_c), jnp.sin(ang_c)], axis=-1)
    cos = jnp.concatenate([jnp.ones((lead, hd), F32), cos], axis=0)
    sin = jnp.concatenate([jnp.zeros((lead, hd), F32), sin], axis=0)
    return cos, sin


def kernel(x_prompt, x_sample, cache_k, cache_v, state_gla, c, c_ctx, norm1_w, norm2_w, w_ada, b_ada, w_in,
           q_norm_w, k_norm_w, gla_up, gla_bias, gla_norm_w, w_out, ffn_w1, ffn_w3, ffn_w2, router_w, moe_w1,
           moe_w3, moe_w2):
    batch, seq1, d = x_prompt.shape
    dec_batch, seq2, _ = x_sample.shape
    depth = w_in.shape[0]
    hd = q_norm_w.shape[-1]
    n_kv = cache_k.shape[3]
    n_q = d // 2 // hd
    heads, dk, dv = state_gla.shape[3:]
    rank = gla_up.shape[2]
    n_experts = router_w.shape[-1]
    attn_w, kv_w, key_w, gla_w = n_q * hd, n_kv * hd, heads * dk, heads * dv
    main_cols = attn_w + 2 * kv_w + 2 * key_w + 2 * gla_w
    q_col0 = attn_w + 2 * kv_w
    k_col0 = q_col0 + key_w
    v_col0 = k_col0 + key_w
    g_col0 = v_col0 + gla_w
    n1, n2 = batch * seq1, dec_batch * seq2
    n = n1 + n2
    assert hd == LANES and n1 % seq2 == 0 and 2 * rank <= LANES and n_experts <= LANES

    def row_group(m, tm):
        tok = m * tm
        return jnp.where(tok < n1, 0, 1 + (tok - n1) // seq2)

    def table_block(m, tm):
        tok = m * tm
        return jnp.where(tok < n1, 0, 1 + ((tok - n1) % seq2) // tm)

    n_rows = -(-(1 + dec_batch) // SUBLANES) * SUBLANES
    cond = jnp.zeros((n_rows, d), F32).at[0].set(c_ctx).at[1:1 + dec_batch].set(c)
    mod_all = ada_modulation(cond, w_ada, b_ada)

    tm_row = _tile(math.gcd(n1, seq2), 256)
    tm_prep = _tile(math.gcd(n1, seq2), 512)
    tm_mm = _tile(math.gcd(n1, seq2), 1024)
    cos, sin = _rope_tables(seq2, hd, tm_prep)

    h = jnp.concatenate([x_prompt.reshape(n1, d), x_sample.reshape(n2, d)], axis=0)
    new_k, new_v, new_s = [], [], None
    for l in range(depth):
        mod = mod_all[l].reshape(n_rows, 1, N_MOD * d)
        w_in_l = w_in[l]
        w_main = w_in_l[:, :main_cols].astype(BF16)
        w_lr = jnp.zeros((d, LANES), BF16).at[:, :2 * rank].set(w_in_l[:, main_cols:].astype(BF16))
        up = jnp.zeros((LANES, 2 * key_w), F32)
        up = up.at[:rank, :key_w].set(gla_up[l, 0]).at[rank:2 * rank, key_w:].set(gla_up[l, 1])
        bias = gla_bias[l].reshape(1, 2 * key_w)

        u, la = norm_modulate(h, norm1_w[l], mod, 1, 0, row_group, tm_row, mode="decay", extra=(w_lr, up, bias))
        z = matmul(u, [w_main], mode="plain", out_dtype=BF16, tm=tm_mm, tn=_tile(main_cols, 1024), tk=d)

        qk, kn = qk_prepare(z, q_norm_w[l], k_norm_w[l], cos, sin, table_block, tm_prep, n_q=n_q, n_kv=n_kv, hd=hd)
        new_k.append(kn[:n1].reshape(batch, seq1, n_kv, hd))
        new_v.append(z[:n1, attn_w + kv_w:attn_w + 2 * kv_w].astype(F32).reshape(batch, seq1, n_kv, hd))

        attn_kw = dict(n_q=n_q, n_kv=n_kv, hd=hd, v_col0=attn_w + kv_w, mix_shape=(n, attn_w + gla_w))
        mix = attention(qk, z, 0, batch, seq1, mix=None, **attn_kw)
        mix = attention(qk, z, n1, dec_batch, seq2, mix=mix, ctx=(cache_k, cache_v, l), **attn_kw)
        gla_kw = dict(heads=heads, dk=dk, dv=dv, q_col0=q_col0, k_col0=k_col0, v_col0=v_col0, g_col0=g_col0)
        mix, new_s = gla(z, la, gla_norm_w[l], 0, batch, seq1, mix, attn_w, new_state=(new_s, l, depth), **gla_kw)
        mix, _ = gla(z, la, gla_norm_w[l], n1, dec_batch, seq2, mix, attn_w, s0=(state_gla, l), **gla_kw)

        h = matmul(mix, [w_out[l].astype(BF16)], mode="resid", out_dtype=F32, tm=tm_mm, tn=_tile(d, 1024), tk=d,
                   h=h, mod=mod, gate_col=2 * d, row_group=row_group)

        i = l // 2
        if l % 2 == 0:
            (u2,) = norm_modulate(h, norm2_w[l], mod, 4, 3, row_group, tm_row, mode="plain", extra=())
            w1, w3, w2 = ffn_w1[i].astype(BF16), ffn_w3[i].astype(BF16), ffn_w2[i].astype(BF16)
            hid = matmul(u2, [w1, w3], mode="swiglu", out_dtype=BF16, tm=tm_mm, tn=_tile(w1.shape[-1], 512), tk=d)
            h = matmul(hid, [w2], mode="resid", out_dtype=F32, tm=tm_mm, tn=_tile(d, 1024),
                       tk=_tile(hid.shape[1], 2048), h=h, mod=mod, gate_col=5 * d, row_group=row_group)
        else:
            rw = jnp.zeros((d, LANES), F32).at[:, :n_experts].set(router_w[i])
            xp, meta, counts = norm_modulate(h, norm2_w[l], mod, 4, 3, row_group, tm_row, mode="router",
                                             extra=(rw,), n_experts=n_experts)
            h = moe_ffn(xp, meta, counts, h, mod, 5 * d, row_group, moe_w1[i].astype(BF16), moe_w3[i].astype(BF16),
                        moe_w2[i].astype(BF16), tm_prep, tm_prep)

    y_prompt = h[:n1].reshape(batch, seq1, d)
    y_sample = h[n1:].reshape(dec_batch, seq2, d)
    return (y_prompt, y_sample, jnp.stack(new_k, axis=1), jnp.stack(new_v, axis=1),
            new_s.astype(x_prompt.dtype))
```

```python
import functools
import math

import jax
import jax.numpy as jnp
from jax import lax
from jax.experimental import pallas as pl
from jax.experimental.pallas import tpu as pltpu

F32 = jnp.float32
BF16 = jnp.bfloat16
U32 = jnp.uint32

EPS = 1e-6
GRID_W = 64
GLA_CHUNK = 64
GLA_TAU = 16.0
ROPE_THETA = 10000.0
N_MOD = 6
LANES = 128
SUBLANES = 8
VMEM_LIMIT_BYTES = 56 * 1024 * 1024


def _params(*semantics):
    return pltpu.CompilerParams(dimension_semantics=semantics, vmem_limit_bytes=VMEM_LIMIT_BYTES)


def _tile(n, cap):
    t = cap
    while t > 1 and n % t:
        t //= 2
    return t


def _dot(a, b):
    return jnp.dot(a, b, preferred_element_type=F32)


def _dot_nt(a, b):
    return lax.dot_general(a, b, (((1,), (1,)), ((), ())), preferred_element_type=F32)


def _dot_tn(a, b):
    return lax.dot_general(a, b, (((0,), (0,)), ((), ())), preferred_element_type=F32)


def _split_bf16(x):
    hi = x.astype(BF16)
    lo = (x - hi.astype(F32)).astype(BF16)
    return hi, lo


def _silu(x):
    return x * jax.nn.sigmoid(x)


HI_MASK = 0xFFFF0000


def _pack_pair(lo, hi):
    lo_bits = lax.bitcast_convert_type(lo.astype(BF16).astype(F32), U32)
    hi_bits = lax.bitcast_convert_type(hi.astype(BF16).astype(F32), U32)
    return (lo_bits >> 16) | (hi_bits & jnp.uint32(HI_MASK))


def _store_row_chunks(ref, chunk0, packed):
    for c in range(packed.shape[1] // LANES):
        ref[:, chunk0 + c, :] = packed[:, c * LANES:(c + 1) * LANES]


def _unpack_pair(p):
    lo = lax.bitcast_convert_type(p << 16, F32)
    hi = lax.bitcast_convert_type(p & jnp.uint32(HI_MASK), F32)
    return lo, hi


def _ada_kernel(c_ref, w_ref, b_ref, o_ref):
    s = _silu(c_ref[...]).astype(BF16)
    o_ref[...] = _dot(s, w_ref[...].astype(BF16)) + b_ref[...]


def ada_modulation(cond, w_ada, b_ada):
    depth, d, n = w_ada.shape
    r = cond.shape[0]
    tn = _tile(n, 1024)
    return pl.pallas_call(
        _ada_kernel,
        grid=(depth, n // tn),
        in_specs=[
            pl.BlockSpec((r, d), lambda l, j: (0, 0)),
            pl.BlockSpec((None, d, tn), lambda l, j: (l, 0, j)),
            pl.BlockSpec((None, 1, tn), lambda l, j: (l, 0, j)),
        ],
        out_specs=pl.BlockSpec((None, r, tn), lambda l, j: (l, 0, j)),
        out_shape=jax.ShapeDtypeStruct((depth, r, n), F32),
        compiler_params=_params("arbitrary", "arbitrary"),
        name="ada",
    )(cond, w_ada, b_ada.reshape(depth, 1, n))


def _norm_kernel(h_ref, nw_ref, sc_ref, sh_ref, *rest, mode, n_experts):
    x = h_ref[...]
    ms = jnp.mean(x * x, axis=-1, keepdims=True)
    y = x * lax.rsqrt(ms + EPS) * nw_ref[...]
    u = y * (1.0 + sc_ref[...]) + sh_ref[...]
    if mode == "decay":
        wlr_ref, up_ref, bias_ref, u_ref, la_ref = rest
        ub = u.astype(BF16)
        u_ref[...] = ub
        lr = _dot(ub, wlr_ref[...])
        lr_hi, lr_lo = _split_bf16(lr)
        up_hi, up_lo = _split_bf16(up_ref[...])
        logits = _dot(lr_hi, up_hi) + (_dot(lr_lo, up_hi) + _dot(lr_hi, up_lo)) + bias_ref[...]
        log_sig = jnp.minimum(logits, 0.0) - jnp.log1p(jnp.exp(-jnp.abs(logits)))
        la = log_sig * (1.0 / GLA_TAU)
        tm = la.shape[0]
        half = la.shape[1] // 2
        t_row = lax.broadcasted_iota(jnp.int32, (tm, tm), 0)
        t_col = lax.broadcasted_iota(jnp.int32, (tm, tm), 1)
        same_chunk = (t_row // GLA_CHUNK) == (t_col // GLA_CHUNK)
        tri_f = jnp.logical_and(same_chunk, t_col <= t_row).astype(BF16)
        tri_b = jnp.logical_and(same_chunk, t_col >= t_row).astype(BF16)
        la_hi, la_lo = _split_bf16(la)
        la_ref[:, :half] = _dot(tri_f, la_hi[:, :half]) + _dot(tri_f, la_lo[:, :half])
        la_ref[:, half:] = _dot(tri_b, la_hi[:, half:]) + _dot(tri_b, la_lo[:, half:])
    elif mode == "plain":
        (u_ref,) = rest
        u_ref[...] = u.astype(BF16)
    else:
        rw_ref, xp_ref, meta_ref, cnt_ref, run_ref = rest
        half = u.shape[1] // 2
        _store_row_chunks(xp_ref, 0, _pack_pair(u[:, :half], u[:, half:]))
        logits = jnp.dot(u, rw_ref[...], preferred_element_type=F32, precision=lax.Precision.HIGHEST)
        lane = lax.broadcasted_iota(jnp.int32, logits.shape, 1)
        neg = jnp.float32(-jnp.inf)
        logits = jnp.where(lane < n_experts, logits, neg)
        v1 = jnp.max(logits, axis=-1, keepdims=True)
        i1 = jnp.min(jnp.where(logits == v1, lane, LANES), axis=-1, keepdims=True)
        rest_l = jnp.where(lane == i1, neg, logits)
        v2 = jnp.max(rest_l, axis=-1, keepdims=True)
        i2 = jnp.min(jnp.where(rest_l == v2, lane, LANES), axis=-1, keepdims=True)
        e2 = jnp.exp(v2 - v1)
        g1 = 1.0 / (1.0 + e2)
        g2 = e2 / (1.0 + e2)

        @pl.when(pl.program_id(0) == 0)
        def _():
            run_ref[...] = jnp.zeros_like(run_ref)

        tm = logits.shape[0]
        chosen = jnp.logical_or(lane == i1, lane == i2)
        t_row = lax.broadcasted_iota(jnp.int32, (tm, tm), 0)
        t_col = lax.broadcasted_iota(jnp.int32, (tm, tm), 1)
        before = _dot((t_col < t_row).astype(BF16), chosen.astype(BF16)) + run_ref[...]
        r1 = jnp.sum(jnp.where(lane == i1, before, 0.0), axis=-1, keepdims=True)
        r2 = jnp.sum(jnp.where(lane == i2, before, 0.0), axis=-1, keepdims=True)
        run_ref[...] += jnp.sum(chosen.astype(F32), axis=0, keepdims=True)
        cnt_ref[...] = jnp.broadcast_to(run_ref[...], cnt_ref.shape)
        fields = (i1.astype(F32), i2.astype(F32), g1, g2, r1, r2)
        meta = jnp.zeros(logits.shape, F32)
        for pos, val in enumerate(fields):
            meta = jnp.where(lane == pos, val, meta)
        meta_ref[...] = meta


def norm_modulate(h, norm_w, mod, sc_col, sh_col, row_group, tm, *, mode, extra, n_experts=0):
    n, d = h.shape
    in_specs = [
        pl.BlockSpec((tm, d), lambda m: (m, 0)),
        pl.BlockSpec((1, d), lambda m: (0, 0)),
        pl.BlockSpec((None, 1, d), lambda m: (row_group(m, tm), 0, sc_col)),
        pl.BlockSpec((None, 1, d), lambda m: (row_group(m, tm), 0, sh_col)),
    ]
    if mode == "decay":
        wlr, up, bias = extra
        in_specs += [
            pl.BlockSpec(wlr.shape, lambda m: (0, 0)),
            pl.BlockSpec(up.shape, lambda m: (0, 0)),
            pl.BlockSpec(bias.shape, lambda m: (0, 0)),
        ]
        side = up.shape[1]
    elif mode == "router":
        (rw,) = extra
        in_specs += [pl.BlockSpec(rw.shape, lambda m: (0, 0))]
    scratch = []
    if mode == "router":
        chunks = d // 2 // LANES
        out_specs = [pl.BlockSpec((tm, chunks, LANES), lambda m: (m, 0, 0)),
                     pl.BlockSpec((tm, LANES), lambda m: (m, 0)),
                     pl.BlockSpec((SUBLANES, LANES), lambda m: (0, 0))]
        out_shape = [jax.ShapeDtypeStruct((n, chunks, LANES), U32), jax.ShapeDtypeStruct((n, LANES), F32),
                     jax.ShapeDtypeStruct((SUBLANES, LANES), F32)]
        scratch = [pltpu.VMEM((1, LANES), F32)]
    else:
        out_specs = [pl.BlockSpec((tm, d), lambda m: (m, 0))]
        out_shape = [jax.ShapeDtypeStruct((n, d), BF16)]
        if mode == "decay":
            out_specs += [pl.BlockSpec((tm, side), lambda m: (m, 0))]
            out_shape += [jax.ShapeDtypeStruct((n, side), F32)]
    return pl.pallas_call(
        functools.partial(_norm_kernel, mode=mode, n_experts=n_experts),
        grid=(n // tm,),
        in_specs=in_specs,
        out_specs=out_specs,
        out_shape=out_shape,
        scratch_shapes=scratch,
        compiler_params=_params("arbitrary"),
        name="norm_" + mode,
    )(h, norm_w.reshape(1, d), mod, mod, *extra)


def _mm_kernel(*refs, mode, n_w):
    x_ref = refs[0]
    w_refs = refs[1:1 + n_w]
    wb_refs = refs[len(refs) - n_w:]
    rest = refs[1 + n_w:len(refs) - n_w]

    @pl.when(pl.program_id(1) == 0)
    def _():
        for w_ref, wb_ref in zip(w_refs, wb_refs):
            wb_ref[...] = w_ref[...].astype(BF16)

    x = x_ref[...]
    if mode == "swiglu":
        (o_ref,) = rest
        o_ref[...] = (_silu(_dot(x, wb_refs[0][...])) * _dot(x, wb_refs[1][...])).astype(o_ref.dtype)
    elif mode == "resid":
        h_ref, g_ref, o_ref = rest
        o_ref[...] = h_ref[...] + g_ref[...] * _dot(x, wb_refs[0][...])
    else:
        (o_ref,) = rest
        o_ref[...] = _dot(x, wb_refs[0][...]).astype(o_ref.dtype)


def _mm_tiles(m_cap, k, n_cols, n_w, mode):
    tn = _tile(n_cols, 512 if (n_w > 1 or k > 4096) else 1024)
    tm = _tile(m_cap, 512 if (mode == "resid" or k > 4096) else 1024)
    io_bytes = {"plain": 2, "swiglu": 2, "resid": 8}[mode]
    est = n_w * k * tn * (4 + 2) + 2 * tm * k * 2 + 2 * tm * tn * io_bytes + tm * tn * 4 * n_w
    assert est <= VMEM_LIMIT_BYTES, (est, tm, tn)
    return tm, tn


def matmul(x, ws, layer, *, mode, out_dtype, m_cap, n_cols=None, h=None, mod=None, gate_col=None, row_group=None):
    m_tot, k = x.shape
    n_cols = ws[0].shape[2] if n_cols is None else n_cols
    tm, tn = _mm_tiles(m_cap, k, n_cols, len(ws), mode)
    w_spec = pl.BlockSpec((None, k, tn), lambda n, m: (layer, 0, n), pipeline_mode=pl.Buffered(1))
    in_specs = [pl.BlockSpec((tm, k), lambda n, m: (m, 0))] + [w_spec] * len(ws)
    args = [x, *ws]
    if mode == "resid":
        gate_blk = gate_col // tn
        in_specs += [
            pl.BlockSpec((tm, tn), lambda n, m: (m, n)),
            pl.BlockSpec((None, 1, tn), lambda n, m: (row_group(m, tm), 0, gate_blk + n)),
        ]
        args += [h, mod]
    return pl.pallas_call(
        functools.partial(_mm_kernel, mode=mode, n_w=len(ws)),
        grid=(n_cols // tn, m_tot // tm),
        in_specs=in_specs,
        out_specs=pl.BlockSpec((tm, tn), lambda n, m: (m, n)),
        out_shape=jax.ShapeDtypeStruct((m_tot, n_cols), out_dtype),
        scratch_shapes=[pltpu.VMEM((k, tn), BF16)] * len(ws),
        compiler_params=_params("arbitrary", "arbitrary"),
        name="mm_" + mode,
    )(*args)


def _cast_kernel(x_ref, o_ref):
    o_ref[...] = x_ref[...].astype(o_ref.dtype)


def cast_bf16(w, layer):
    _, n_e, k, f = w.shape
    tk = _tile(k, max(SUBLANES, 2 * 1024 * 1024 // f))
    return pl.pallas_call(
        _cast_kernel,
        grid=(n_e, k // tk),
        in_specs=[pl.BlockSpec((None, None, tk, f), lambda e, i: (layer, e, i, 0))],
        out_specs=pl.BlockSpec((None, tk, f), lambda e, i: (e, i, 0)),
        out_shape=jax.ShapeDtypeStruct(w.shape[1:], BF16),
        compiler_params=_params("arbitrary", "arbitrary"),
        name="cast_bf16",
    )(w)


def _rowcopy_kernel(src_ref, dst_ref, x_hbm, *rest, rows):
    o_hbm, sem = rest[-2:]

    def issue(r, carry):
        pltpu.make_async_copy(x_hbm.at[pl.ds(src_ref[0, r], 1)], o_hbm.at[pl.ds(dst_ref[0, r], 1)], sem).start()
        return carry

    lax.fori_loop(0, rows, issue, 0)

    def drain(r, carry):
        pltpu.make_async_copy(x_hbm.at[pl.ds(0, 1)], o_hbm.at[pl.ds(0, 1)], sem).wait()
        return carry

    lax.fori_loop(0, rows, drain, 0)


def row_copy(x, src, dst, n_out, init=None):
    total = src.shape[0]
    rows = _tile(total, 2048)
    idx_spec = pl.BlockSpec((None, 1, rows), lambda g: (g, 0, 0), memory_space=pltpu.SMEM)
    any_spec = pl.BlockSpec(memory_space=pl.ANY)
    args = [src.reshape(total // rows, 1, rows), dst.reshape(total // rows, 1, rows), x]
    in_specs = [idx_spec, idx_spec, any_spec]
    aliases = {}
    if init is not None:
        args.append(init)
        in_specs.append(any_spec)
        aliases = {3: 0}
    return pl.pallas_call(
        functools.partial(_rowcopy_kernel, rows=rows),
        grid=(total // rows,),
        in_specs=in_specs,
        out_specs=any_spec,
        out_shape=jax.ShapeDtypeStruct((n_out, *x.shape[1:]), x.dtype),
        scratch_shapes=[pltpu.SemaphoreType.DMA(())],
        input_output_aliases=aliases,
        compiler_params=_params("arbitrary"),
        name="row_copy",
    )(*args)


def _expert_swiglu_kernel(te_ref, na_ref, xs_ref, w1_ref, w3_ref, o_ref, lo_ref, hi_ref):
    i = pl.program_id(0)
    active = i < na_ref[0]
    half = lo_ref.shape[1]

    @pl.when(jnp.logical_and(active, pl.program_id(1) == 0))
    def _():
        for c in range(xs_ref.shape[1]):
            lo, hi = _unpack_pair(xs_ref[:, c, :])
            lo_ref[:, c * LANES:(c + 1) * LANES] = lo.astype(BF16)
            hi_ref[:, c * LANES:(c + 1) * LANES] = hi.astype(BF16)

    @pl.when(active)
    def _():
        a = _dot(lo_ref[...], w1_ref[:half, :]) + _dot(hi_ref[...], w1_ref[half:, :])
        b = _dot(lo_ref[...], w3_ref[:half, :]) + _dot(hi_ref[...], w3_ref[half:, :])
        o_ref[...] = (_silu(a) * b).astype(o_ref.dtype)

    @pl.when(jnp.logical_not(active))
    def _():
        o_ref[...] = jnp.zeros_like(o_ref)


def _expert_down_kernel(te_ref, na_ref, x_ref, wlo_ref, whi_ref, o_ref):
    active = pl.program_id(0) < na_ref[0]

    @pl.when(active)
    def _():
        x = x_ref[...]
        _store_row_chunks(o_ref, 0, _pack_pair(_dot(x, wlo_ref[...]), _dot(x, whi_ref[...])))

    @pl.when(jnp.logical_not(active))
    def _():
        o_ref[...] = jnp.zeros_like(o_ref)


def expert_ffn(xs, w1, w3, w2, tile_expert, n_active, tm):
    p_rows, chunks, _ = xs.shape
    half = chunks * LANES
    n_e, d, f = w1.shape
    tn = _tile(f, 512)
    nf = f // tn

    def w13_map(i, j, te, na):
        return te[i], 0, jnp.where(i < na[0], j, nf - 1)

    hid = pl.pallas_call(
        _expert_swiglu_kernel,
        grid_spec=pltpu.PrefetchScalarGridSpec(
            num_scalar_prefetch=2,
            grid=(p_rows // tm, nf),
            in_specs=[
                pl.BlockSpec((tm, chunks, LANES), lambda i, j, te, na: (i, 0, 0)),
                pl.BlockSpec((None, d, tn), w13_map),
                pl.BlockSpec((None, d, tn), w13_map),
            ],
            out_specs=pl.BlockSpec((tm, tn), lambda i, j, te, na: (i, j)),
            scratch_shapes=[pltpu.VMEM((tm, half), BF16), pltpu.VMEM((tm, half), BF16)],
        ),
        out_shape=jax.ShapeDtypeStruct((p_rows, f), BF16),
        compiler_params=_params("arbitrary", "arbitrary"),
        name="expert_swiglu",
    )(tile_expert, n_active, xs, w1, w3)

    cpb = min(SUBLANES, chunks)
    tn2 = cpb * LANES
    nd = half // tn2

    def w2_map(upper, i, j, te, na):
        return te[i], 0, upper * nd + jnp.where(i < na[0], j, nd - 1)

    return pl.pallas_call(
        _expert_down_kernel,
        grid_spec=pltpu.PrefetchScalarGridSpec(
            num_scalar_prefetch=2,
            grid=(p_rows // tm, nd),
            in_specs=[
                pl.BlockSpec((tm, f), lambda i, j, te, na: (i, 0)),
                pl.BlockSpec((None, f, tn2), functools.partial(w2_map, 0)),
                pl.BlockSpec((None, f, tn2), functools.partial(w2_map, 1)),
            ],
            out_specs=pl.BlockSpec((tm, cpb, LANES), lambda i, j, te, na: (i, j, 0)),
        ),
        out_shape=jax.ShapeDtypeStruct((p_rows, chunks, LANES), U32),
        compiler_params=_params("arbitrary", "arbitrary"),
        name="expert_down",
    )(tile_expert, n_active, hid, w2, w2)


def _combine_kernel(a_ref, b_ref, meta_ref, h_ref, g_ref, o_ref):
    meta = meta_ref[...]
    lane = lax.broadcasted_iota(jnp.int32, meta.shape, 1)
    g1 = jnp.sum(jnp.where(lane == 2, meta, 0.0), axis=-1, keepdims=True)
    g2 = jnp.sum(jnp.where(lane == 3, meta, 0.0), axis=-1, keepdims=True)
    chunks = a_ref.shape[1]
    half = chunks * LANES
    for c in range(chunks):
        a_lo, a_hi = _unpack_pair(a_ref[:, c, :])
        b_lo, b_hi = _unpack_pair(b_ref[:, c, :])
        lo = slice(c * LANES, (c + 1) * LANES)
        hi = slice(half + c * LANES, half + (c + 1) * LANES)
        o_ref[:, lo] = h_ref[:, lo] + g_ref[:, lo] * (g1 * a_lo + g2 * b_lo)
        o_ref[:, hi] = h_ref[:, hi] + g_ref[:, hi] * (g1 * a_hi + g2 * b_hi)


def combine_experts(yg, meta, h, mod, gate_col, row_group, tm):
    n, d = h.shape
    chunks = yg.shape[2]
    return pl.pallas_call(
        _combine_kernel,
        grid=(n // tm,),
        in_specs=[
            pl.BlockSpec((tm, None, chunks, LANES), lambda m: (m, 0, 0, 0)),
            pl.BlockSpec((tm, None, chunks, LANES), lambda m: (m, 1, 0, 0)),
            pl.BlockSpec((tm, LANES), lambda m: (m, 0)),
            pl.BlockSpec((tm, d), lambda m: (m, 0)),
            pl.BlockSpec((None, 1, d), lambda m: (row_group(m, tm), 0, gate_col // d)),
        ],
        out_specs=pl.BlockSpec((tm, d), lambda m: (m, 0)),
        out_shape=jax.ShapeDtypeStruct((n, d), F32),
        compiler_params=_params("arbitrary"),
        name="combine",
    )(yg, yg, meta, h, mod)


def moe_ffn(xp, meta, counts, h, mod, gate_col, row_group, w1, w3, w2, tm_e, tm_c):
    n, d = h.shape
    n_e = w1.shape[0]
    experts = meta[:, 0:2].astype(jnp.int32)
    ranks = meta[:, 4:6].astype(jnp.int32)
    cnt = counts[0, :n_e].astype(jnp.int32)
    seg = (cnt + tm_e - 1) // tm_e * tm_e
    seg_end = jnp.cumsum(seg)
    seg_start = seg_end - seg
    onehot = experts[:, :, None] == jnp.arange(n_e)[None, None, :]
    pos = jnp.sum(jnp.where(onehot, seg_start[None, None, :], 0), axis=-1) + ranks
    p_rows = 2 * n + n_e * tm_e
    n_tiles = p_rows // tm_e
    n_active = (seg_end[-1] // tm_e).reshape(1)
    tile_row = jnp.arange(n_tiles) * tm_e
    tile_expert = jnp.sum(tile_row[:, None] >= seg_end[None, :], axis=1)
    last = jnp.sum(tile_expert * (jnp.arange(n_tiles) == n_active[0] - 1))
    tile_expert = jnp.where(jnp.arange(n_tiles) < n_active[0], tile_expert, last).astype(jnp.int32)

    tok2 = jnp.repeat(jnp.arange(n, dtype=jnp.int32), 2)
    flat = pos.reshape(-1).astype(jnp.int32)
    xs = row_copy(xp, tok2, flat, p_rows, init=jnp.zeros((p_rows, *xp.shape[1:]), U32))
    ys = expert_ffn(xs, w1, w3, w2, tile_expert, n_active.astype(jnp.int32), tm_e)
    yg = row_copy(ys, flat, jnp.arange(2 * n, dtype=jnp.int32), 2 * n).reshape(n, 2, *xp.shape[1:])
    return combine_experts(yg, meta, h, mod, gate_col, row_group, tm_c)


def _qkprep_kernel(z_ref, qw_ref, kw_ref, cos_ref, sin_ref, qk_ref, kn_ref, *, n_q, n_kv, hd, scale):
    cos = cos_ref[...]
    sin = sin_ref[...]
    lane = lax.broadcasted_iota(jnp.int32, cos.shape, 1)
    quarter = hd // 4
    first = (lane % (2 * quarter)) < quarter
    for hh in range(n_q + n_kv):
        cols = slice(hh * hd, (hh + 1) * hd)
        x = z_ref[:, cols].astype(F32)
        ms = jnp.mean(x * x, axis=-1, keepdims=True)
        y = x * lax.rsqrt(ms + EPS) * (qw_ref[...] if hh < n_q else kw_ref[...])
        if hh >= n_q:
            kn_ref[:, (hh - n_q) * hd:(hh - n_q + 1) * hd] = y
        partner = jnp.where(first, pltpu.roll(y, hd - quarter, 1), pltpu.roll(y, quarter, 1))
        r = y * cos + partner * sin
        if hh < n_q:
            r = r * scale
        qk_ref[:, cols] = r.astype(BF16)


def qk_prepare(z, q_norm_w, k_norm_w, cos, sin, table_block, tm, *, n_q, n_kv, hd):
    n = z.shape[0]
    w = (n_q + n_kv) * hd
    return pl.pallas_call(
        functools.partial(_qkprep_kernel, n_q=n_q, n_kv=n_kv, hd=hd, scale=hd ** -0.5),
        grid=(n // tm,),
        in_specs=[
            pl.BlockSpec((tm, w), lambda m: (m, 0)),
            pl.BlockSpec((1, hd), lambda m: (0, 0)),
            pl.BlockSpec((1, hd), lambda m: (0, 0)),
            pl.BlockSpec((tm, hd), lambda m: (table_block(m, tm), 0)),
            pl.BlockSpec((tm, hd), lambda m: (table_block(m, tm), 0)),
        ],
        out_specs=[pl.BlockSpec((tm, w), lambda m: (m, 0)), pl.BlockSpec((tm, n_kv * hd), lambda m: (m, 0))],
        out_shape=[jax.ShapeDtypeStruct((n, w), BF16), jax.ShapeDtypeStruct((n, n_kv * hd), F32)],
        compiler_params=_params("arbitrary"),
        name="qk_prep",
    )(z, q_norm_w.reshape(1, hd), k_norm_w.reshape(1, hd), cos, sin)


def _attn_kernel(q_ref, k_ref, v_ref, *rest, group, hd, has_ctx):
    o_ref = rest[-1]
    if has_ctx:
        kc_ref, vc_ref = rest[:2]
        kc = kc_ref[...].astype(BF16)
        vc = vc_ref[...].astype(BF16)
    k = k_ref[...]
    v = v_ref[...]
    for gi in range(group):
        cols = slice(gi * hd, (gi + 1) * hd)
        q = q_ref[:, cols]
        s = _dot_nt(q, k)
        mx = jnp.max(s, axis=-1, keepdims=True)
        if has_ctx:
            sc = _dot_nt(q, kc)
            mx = jnp.maximum(mx, jnp.max(sc, axis=-1, keepdims=True))
        p = jnp.exp(s - mx)
        den = jnp.sum(p, axis=-1, keepdims=True)
        o = _dot(p.astype(BF16), v)
        if has_ctx:
            pc = jnp.exp(sc - mx)
            den = den + jnp.sum(pc, axis=-1, keepdims=True)
            o = o + _dot(pc.astype(BF16), vc)
        o_ref[:, cols] = (o * (1.0 / den)).astype(o_ref.dtype)


def _mix_target(mix, mix_shape, args, in_specs):
    if mix is None:
        return jax.ShapeDtypeStruct(mix_shape, BF16), {}
    args.append(mix)
    in_specs.append(pl.BlockSpec(memory_space=pl.ANY))
    return jax.ShapeDtypeStruct(mix.shape, mix.dtype), {len(args) - 1: 0}


def attention(qk, z, row0, n_batch, seq, *, n_q, n_kv, hd, v_col0, mix, mix_shape, ctx=None):
    group = n_q // n_kv
    tq = _tile(seq, 512)
    nq = seq // tq
    rb_q = row0 // tq
    rb_s = row0 // seq
    in_specs = [
        pl.BlockSpec((tq, group * hd), lambda b, g, i: (rb_q + b * nq + i, g)),
        pl.BlockSpec((seq, hd), lambda b, g, i: (rb_s + b, n_q + g)),
        pl.BlockSpec((seq, hd), lambda b, g, i: (rb_s + b, v_col0 // hd + g)),
    ]
    args = [qk, qk, z]
    if ctx is not None:
        cache_k, cache_v, layer = ctx
        past = cache_k.shape[2]
        c_spec = pl.BlockSpec((None, None, past, hd), lambda b, g, i: (b, layer, 0, g))
        in_specs += [c_spec, c_spec]
        args += [cache_k.reshape(*cache_k.shape[:3], n_kv * hd), cache_v.reshape(*cache_v.shape[:3], n_kv * hd)]
    out_shape, aliases = _mix_target(mix, mix_shape, args, in_specs)
    return pl.pallas_call(
        functools.partial(_attn_kernel, group=group, hd=hd, has_ctx=ctx is not None),
        grid=(n_batch, n_kv, nq),
        in_specs=in_specs,
        out_specs=pl.BlockSpec((tq, group * hd), lambda b, g, i: (rb_q + b * nq + i, g)),
        out_shape=out_shape,
        input_output_aliases=aliases,
        compiler_params=_params("arbitrary", "arbitrary", "arbitrary"),
        name="attn_ctx" if ctx is not None else "attn",
    )(*args)


def _gla_kernel(q_ref, k_ref, v_ref, g_ref, cf_ref, cb_ref, nw_ref, *rest, seq, chunk, has_s0, want_state, scale,
                n_aliased):
    rest = list(rest)
    s0_ref = rest.pop(0) if has_s0 else None
    del rest[:n_aliased]
    o_ref = rest.pop(0)
    sout_ref = rest.pop(0) if want_state else None
    qdf_ref, qdb_ref, oin_ref, kvf_ref, kvb_ref, spf_ref, spb_ref = rest
    n = seq // chunk
    unroll = True if n <= 4 else 4

    row = lax.broadcasted_iota(jnp.int32, (chunk, chunk), 0)
    col = lax.broadcasted_iota(jnp.int32, (chunk, chunk), 1)
    lower = col <= row
    upper = col >= row

    def rows(i):
        return pl.ds(pl.multiple_of(i * chunk, chunk), chunk)

    def intra(i, cum_ref, mask, tot_row, qd_ref, kv_ref):
        sl = rows(i)
        cum = cum_ref[sl, :]
        tot = cum[tot_row:tot_row + 1, :]
        q = q_ref[sl, :].astype(F32) * scale
        k = k_ref[sl, :].astype(F32)
        v = v_ref[sl, :]
        q_dec = (q * jnp.exp(cum)).astype(BF16)
        k_inv = (k * jnp.exp(-cum)).astype(BF16)
        k_end = (k * jnp.exp(tot - cum)).astype(BF16)
        qd_ref[sl, :] = q_dec
        kv_ref[i] = _dot_tn(v, k_end)
        att = jnp.where(mask, _dot_nt(q_dec, k_inv), 0.0).astype(BF16)
        return _dot(att, v)

    def phase1(i, carry):
        oin_ref[rows(i), :] = (intra(i, cf_ref, lower, chunk - 1, qdf_ref, kvf_ref)
                               + intra(i, cb_ref, upper, 0, qdb_ref, kvb_ref))
        return carry

    lax.fori_loop(0, n, phase1, 0, unroll=unroll)

    def scan(reverse, cum_ref, tot_row, kv_ref, sp_ref, init):
        def step(j, st):
            i = n - 1 - j if reverse else j
            tot = cum_ref[pl.ds(i * chunk + tot_row, 1), :]
            sp_ref[i] = st.astype(BF16)
            return jnp.exp(tot) * st + kv_ref[i]

        return lax.fori_loop(0, n, step, init)

    zero = jnp.zeros(kvf_ref.shape[1:], F32)
    st_f = scan(False, cf_ref, chunk - 1, kvf_ref, spf_ref, s0_ref[0].T if has_s0 else zero)
    st_b = scan(True, cb_ref, 0, kvb_ref, spb_ref, s0_ref[1].T if has_s0 else zero)
    if want_state:
        sout_ref[0] = st_f.T
        sout_ref[1] = st_b.T

    def phase3(i, carry):
        sl = rows(i)
        o = oin_ref[sl, :] + _dot_nt(qdf_ref[sl, :], spf_ref[i]) + _dot_nt(qdb_ref[sl, :], spb_ref[i])
        ms = jnp.mean(o * o, axis=-1, keepdims=True)
        y = o * lax.rsqrt(ms + EPS) * nw_ref[...]
        o_ref[sl, :] = (y * _silu(g_ref[sl, :].astype(F32))).astype(o_ref.dtype)
        return carry

    lax.fori_loop(0, n, phase3, 0, unroll=unroll)


def gla(z, la, norm_w, row0, n_batch, seq, mix, mix_col0, *, heads, dk, dv, q_col0, k_col0, v_col0, g_col0, s0=None,
        new_state=None):
    rb = row0 // seq
    n_chunks = seq // GLA_CHUNK
    want_state = new_state is not None
    in_specs = [
        pl.BlockSpec((seq, dk), lambda b, hh: (rb + b, q_col0 // dk + hh)),
        pl.BlockSpec((seq, dk), lambda b, hh: (rb + b, k_col0 // dk + hh)),
        pl.BlockSpec((seq, dv), lambda b, hh: (rb + b, v_col0 // dv + hh)),
        pl.BlockSpec((seq, dv), lambda b, hh: (rb + b, g_col0 // dv + hh)),
        pl.BlockSpec((seq, dk), lambda b, hh: (rb + b, hh)),
        pl.BlockSpec((seq, dk), lambda b, hh: (rb + b, heads + hh)),
        pl.BlockSpec((1, dv), lambda b, hh: (0, 0)),
    ]
    args = [z, z, z, z, la, la, norm_w.reshape(1, dv)]
    if s0 is not None:
        state, layer = s0
        in_specs += [pl.BlockSpec((None, None, 2, None, dk, dv), lambda b, hh: (b, layer, 0, hh, 0, 0))]
        args += [state]
    n_fixed = len(args)
    mix_shape, aliases = _mix_target(mix, None, args, in_specs)
    out_specs = [pl.BlockSpec((seq, dv), lambda b, hh: (rb + b, mix_col0 // dv + hh))]
    out_shape = [mix_shape]
    if want_state:
        states, layer_out, depth = new_state
        out_specs += [pl.BlockSpec((None, None, 2, None, dk, dv), lambda b, hh: (b, layer_out, 0, hh, 0, 0))]
        out_shape += [jax.ShapeDtypeStruct((n_batch, depth, 2, heads, dk, dv), F32)]
        if states is not None:
            args.append(states)
            in_specs.append(pl.BlockSpec(memory_space=pl.ANY))
            aliases[len(args) - 1] = 1
    res = pl.pallas_call(
        functools.partial(_gla_kernel, seq=seq, chunk=GLA_CHUNK, has_s0=s0 is not None, want_state=want_state,
                          scale=dk ** -0.5, n_aliased=len(args) - n_fixed),
        grid=(n_batch, heads),
        in_specs=in_specs,
        out_specs=out_specs,
        out_shape=out_shape,
        input_output_aliases=aliases,
        scratch_shapes=[pltpu.VMEM((seq, dk), BF16), pltpu.VMEM((seq, dk), BF16), pltpu.VMEM((seq, dv), F32),
                        pltpu.VMEM((n_chunks, dv, dk), F32), pltpu.VMEM((n_chunks, dv, dk), F32),
                        pltpu.VMEM((n_chunks, dv, dk), BF16), pltpu.VMEM((n_chunks, dv, dk), BF16)],
        compiler_params=_params("arbitrary", "arbitrary"),
        name="gla_state" if want_state else "gla",
    )(*args)
    return res if want_state else (res[0], None)


def _rope_tables(seq, hd, lead):
    axis_dim = hd // 2
    t = jnp.arange(seq)
    rowp = (t // GRID_W).astype(F32)
    colp = (t % GRID_W).astype(F32)
    inv = ROPE_THETA ** (-jnp.arange(axis_dim // 2, dtype=F32) * 2.0 / axis_dim)
    ang_r = rowp[:, None] * inv
    ang_c = colp[:, None] * inv
    cos = jnp.concatenate([jnp.cos(ang_r), jnp.cos(ang_r), jnp.cos(ang_c), jnp.cos(ang_c)], axis=-1)
    sin = jnp.concatenate([-jnp.sin(ang_r), jnp.sin(ang_r), -jnp.sin(ang_c), jnp.sin(ang_c)], axis=-1)
    cos = jnp.concatenate([jnp.ones((lead, hd), F32), cos], axis=0)
    sin = jnp.concatenate([jnp.zeros((lead, hd), F32), sin], axis=0)
    return cos, sin


def kernel(x_prompt, x_sample, cache_k, cache_v, state_gla, c, c_ctx, norm1_w, norm2_w, w_ada, b_ada, w_in,
           q_norm_w, k_norm_w, gla_up, gla_bias, gla_norm_w, w_out, ffn_w1, ffn_w3, ffn_w2, router_w, moe_w1,
           moe_w3, moe_w2):
    batch, seq1, d = x_prompt.shape
    dec_batch, seq2, _ = x_sample.shape
    depth = w_in.shape[0]
    hd = q_norm_w.shape[-1]
    n_kv = cache_k.shape[3]
    n_q = d // 2 // hd
    heads, dk, dv = state_gla.shape[3:]
    rank = gla_up.shape[2]
    n_experts = router_w.shape[-1]
    attn_w, kv_w, key_w, gla_w = n_q * hd, n_kv * hd, heads * dk, heads * dv
    main_cols = attn_w + 2 * kv_w + 2 * key_w + 2 * gla_w
    q_col0 = attn_w + 2 * kv_w
    k_col0 = q_col0 + key_w
    v_col0 = k_col0 + key_w
    g_col0 = v_col0 + gla_w
    n1, n2 = batch * seq1, dec_batch * seq2
    n = n1 + n2
    assert hd == LANES and n1 % seq2 == 0 and 2 * rank <= LANES and n_experts <= LANES

    def row_group(m, tm):
        tok = m * tm
        return jnp.where(tok < n1, 0, 1 + (tok - n1) // seq2)

    def table_block(m, tm):
        tok = m * tm
        return jnp.where(tok < n1, 0, 1 + ((tok - n1) % seq2) // tm)

    n_rows = -(-(1 + dec_batch) // SUBLANES) * SUBLANES
    cond = jnp.zeros((n_rows, d), F32).at[0].set(c_ctx).at[1:1 + dec_batch].set(c)
    mod_all = ada_modulation(cond, w_ada, b_ada)

    tm_row = _tile(math.gcd(n1, seq2), 256)
    tm_prep = _tile(math.gcd(n1, seq2), 512)
    m_cap = math.gcd(n1, seq2)
    cos, sin = _rope_tables(seq2, hd, tm_prep)

    h = jnp.concatenate([x_prompt.reshape(n1, d), x_sample.reshape(n2, d)], axis=0)
    new_k, new_v, new_s = [], [], None
    for l in range(depth):
        mod = mod_all[l].reshape(n_rows, 1, N_MOD * d)
        w_lr = jnp.zeros((d, LANES), BF16).at[:, :2 * rank].set(w_in[l, :, main_cols:].astype(BF16))
        up = jnp.zeros((LANES, 2 * key_w), F32)
        up = up.at[:rank, :key_w].set(gla_up[l, 0]).at[rank:2 * rank, key_w:].set(gla_up[l, 1])
        bias = gla_bias[l].reshape(1, 2 * key_w)

        u, la = norm_modulate(h, norm1_w[l], mod, 1, 0, row_group, tm_row, mode="decay", extra=(w_lr, up, bias))
        z = matmul(u, [w_in], l, mode="plain", out_dtype=BF16, m_cap=m_cap, n_cols=main_cols)

        qk, kn = qk_prepare(z, q_norm_w[l], k_norm_w[l], cos, sin, table_block, tm_prep, n_q=n_q, n_kv=n_kv, hd=hd)
        new_k.append(kn[:n1].reshape(batch, seq1, n_kv, hd))
        new_v.append(z[:n1, attn_w + kv_w:attn_w + 2 * kv_w].astype(F32).reshape(batch, seq1, n_kv, hd))

        attn_kw = dict(n_q=n_q, n_kv=n_kv, hd=hd, v_col0=attn_w + kv_w, mix_shape=(n, attn_w + gla_w))
        mix = attention(qk, z, 0, batch, seq1, mix=None, **attn_kw)
        mix = attention(qk, z, n1, dec_batch, seq2, mix=mix, ctx=(cache_k, cache_v, l), **attn_kw)
        gla_kw = dict(heads=heads, dk=dk, dv=dv, q_col0=q_col0, k_col0=k_col0, v_col0=v_col0, g_col0=g_col0)
        mix, new_s = gla(z, la, gla_norm_w[l], 0, batch, seq1, mix, attn_w, new_state=(new_s, l, depth), **gla_kw)
        mix, _ = gla(z, la, gla_norm_w[l], n1, dec_batch, seq2, mix, attn_w, s0=(state_gla, l), **gla_kw)

        h = matmul(mix, [w_out], l, mode="resid", out_dtype=F32, m_cap=m_cap, h=h, mod=mod, gate_col=2 * d,
                   row_group=row_group)

        i = l // 2
        if l % 2 == 0:
            (u2,) = norm_modulate(h, norm2_w[l], mod, 4, 3, row_group, tm_row, mode="plain", extra=())
            hid = matmul(u2, [ffn_w1, ffn_w3], i, mode="swiglu", out_dtype=BF16, m_cap=m_cap)
            h = matmul(hid, [ffn_w2], i, mode="resid", out_dtype=F32, m_cap=m_cap, h=h, mod=mod, gate_col=5 * d,
                       row_group=row_group)
        else:
            rw = jnp.zeros((d, LANES), F32).at[:, :n_experts].set(router_w[i])
            xp, meta, counts = norm_modulate(h, norm2_w[l], mod, 4, 3, row_group, tm_row, mode="router",
                                             extra=(rw,), n_experts=n_experts)
            h = moe_ffn(xp, meta, counts, h, mod, 5 * d, row_group, cast_bf16(moe_w1, i), cast_bf16(moe_w3, i),
                        cast_bf16(moe_w2, i), tm_prep, tm_row)

    y_prompt = h[:n1].reshape(batch, seq1, d)
    y_sample = h[n1:].reshape(dec_batch, seq2, d)
    return (y_prompt, y_sample, jnp.stack(new_k, axis=1), jnp.stack(new_v, axis=1),
            new_s.astype(x_prompt.dtype))
```

```python
import functools
import math

import jax
import jax.numpy as jnp
from jax import lax
from jax.experimental import pallas as pl
from jax.experimental.pallas import tpu as pltpu

F32 = jnp.float32
BF16 = jnp.bfloat16

EPS = 1e-6
GRID_W = 64
GLA_CHUNK = 64
GLA_TAU = 16.0
ROPE_THETA = 10000.0
N_MOD = 6
LANES = 128
SUBLANES = 8
VMEM_LIMIT_BYTES = 56 * 1024 * 1024


def _params(*semantics):
    return pltpu.CompilerParams(dimension_semantics=semantics, vmem_limit_bytes=VMEM_LIMIT_BYTES)


def _tile(n, cap):
    t = cap
    while t > 1 and n % t:
        t //= 2
    return t


def _dot(a, b):
    return jnp.dot(a, b, preferred_element_type=F32)


def _dot_nt(a, b):
    return lax.dot_general(a, b, (((1,), (1,)), ((), ())), preferred_element_type=F32)


def _dot_tn(a, b):
    return lax.dot_general(a, b, (((0,), (0,)), ((), ())), preferred_element_type=F32)


def _split_bf16(x):
    hi = x.astype(BF16)
    lo = (x - hi.astype(F32)).astype(BF16)
    return hi, lo


def _silu(x):
    return x * jax.nn.sigmoid(x)


def _ada_kernel(c_ref, w_ref, b_ref, o_ref):
    s = _silu(c_ref[...]).astype(BF16)
    o_ref[...] = _dot(s, w_ref[...].astype(BF16)) + b_ref[...]


def ada_modulation(cond, w_ada, b_ada):
    depth, d, n = w_ada.shape
    r = cond.shape[0]
    tn = _tile(n, 1024)
    return pl.pallas_call(
        _ada_kernel,
        grid=(depth, n // tn),
        in_specs=[
            pl.BlockSpec((r, d), lambda l, j: (0, 0)),
            pl.BlockSpec((None, d, tn), lambda l, j: (l, 0, j)),
            pl.BlockSpec((None, 1, tn), lambda l, j: (l, 0, j)),
        ],
        out_specs=pl.BlockSpec((None, r, tn), lambda l, j: (l, 0, j)),
        out_shape=jax.ShapeDtypeStruct((depth, r, n), F32),
        compiler_params=_params("arbitrary", "arbitrary"),
        name="ada",
    )(cond, w_ada, b_ada.reshape(depth, 1, n))


def _norm_kernel(h_ref, nw_ref, sc_ref, sh_ref, *rest, mode, n_experts):
    x = h_ref[...]
    ms = jnp.mean(x * x, axis=-1, keepdims=True)
    y = x * lax.rsqrt(ms + EPS) * nw_ref[...]
    u = y * (1.0 + sc_ref[...]) + sh_ref[...]
    if mode == "decay":
        wlr_ref, up_ref, bias_ref, u_ref, la_ref = rest
        ub = u.astype(BF16)
        u_ref[...] = ub
        lr = _dot(ub, wlr_ref[...])
        lr_hi, lr_lo = _split_bf16(lr)
        up_hi, up_lo = _split_bf16(up_ref[...])
        logits = _dot(lr_hi, up_hi) + (_dot(lr_lo, up_hi) + _dot(lr_hi, up_lo)) + bias_ref[...]
        log_sig = jnp.minimum(logits, 0.0) - jnp.log1p(jnp.exp(-jnp.abs(logits)))
        la = log_sig * (1.0 / GLA_TAU)
        tm = la.shape[0]
        half = la.shape[1] // 2
        t_row = lax.broadcasted_iota(jnp.int32, (tm, tm), 0)
        t_col = lax.broadcasted_iota(jnp.int32, (tm, tm), 1)
        same_chunk = (t_row // GLA_CHUNK) == (t_col // GLA_CHUNK)
        tri_f = jnp.logical_and(same_chunk, t_col <= t_row).astype(BF16)
        tri_b = jnp.logical_and(same_chunk, t_col >= t_row).astype(BF16)
        la_hi, la_lo = _split_bf16(la)
        la_ref[:, :half] = _dot(tri_f, la_hi[:, :half]) + _dot(tri_f, la_lo[:, :half])
        la_ref[:, half:] = _dot(tri_b, la_hi[:, half:]) + _dot(tri_b, la_lo[:, half:])
    elif mode == "plain":
        (u_ref,) = rest
        u_ref[...] = u.astype(BF16)
    else:
        rw_ref, u_ref, meta_ref, cnt_ref = rest
        u_ref[...] = u.astype(BF16)
        logits = jnp.dot(u, rw_ref[...], preferred_element_type=F32, precision=lax.Precision.HIGHEST)
        lane = lax.broadcasted_iota(jnp.int32, logits.shape, 1)
        neg = jnp.float32(-jnp.inf)
        logits = jnp.where(lane < n_experts, logits, neg)
        v1 = jnp.max(logits, axis=-1, keepdims=True)
        i1 = jnp.min(jnp.where(logits == v1, lane, LANES), axis=-1, keepdims=True)
        rest_l = jnp.where(lane == i1, neg, logits)
        v2 = jnp.max(rest_l, axis=-1, keepdims=True)
        i2 = jnp.min(jnp.where(rest_l == v2, lane, LANES), axis=-1, keepdims=True)
        e2 = jnp.exp(v2 - v1)
        g1 = 1.0 / (1.0 + e2)
        g2 = e2 / (1.0 + e2)

        tm = logits.shape[0]
        chosen = jnp.logical_or(lane == i1, lane == i2)
        t_row = lax.broadcasted_iota(jnp.int32, (tm, tm), 0)
        t_col = lax.broadcasted_iota(jnp.int32, (tm, tm), 1)
        before = _dot((t_col < t_row).astype(BF16), chosen.astype(BF16))
        r1 = jnp.sum(jnp.where(lane == i1, before, 0.0), axis=-1, keepdims=True)
        r2 = jnp.sum(jnp.where(lane == i2, before, 0.0), axis=-1, keepdims=True)
        cnt_ref[...] = jnp.broadcast_to(jnp.sum(chosen.astype(F32), axis=0, keepdims=True), cnt_ref.shape)
        fields = (i1.astype(F32), i2.astype(F32), g1, g2, r1, r2)
        meta = jnp.zeros(logits.shape, F32)
        for pos, val in enumerate(fields):
            meta = jnp.where(lane == pos, val, meta)
        meta_ref[...] = meta


def norm_modulate(h, norm_w, mod, sc_col, sh_col, row_group, tm, *, mode, extra, n_experts=0):
    n, d = h.shape
    in_specs = [
        pl.BlockSpec((tm, d), lambda m: (m, 0)),
        pl.BlockSpec((1, d), lambda m: (0, 0)),
        pl.BlockSpec((None, 1, d), lambda m: (row_group(m, tm), 0, sc_col)),
        pl.BlockSpec((None, 1, d), lambda m: (row_group(m, tm), 0, sh_col)),
    ]
    if mode == "decay":
        wlr, up, bias = extra
        in_specs += [
            pl.BlockSpec(wlr.shape, lambda m: (0, 0)),
            pl.BlockSpec(up.shape, lambda m: (0, 0)),
            pl.BlockSpec(bias.shape, lambda m: (0, 0)),
        ]
        side = up.shape[1]
    elif mode == "router":
        (rw,) = extra
        in_specs += [pl.BlockSpec(rw.shape, lambda m: (0, 0))]
    if mode == "router":
        out_specs = [pl.BlockSpec((tm, d), lambda m: (m, 0)),
                     pl.BlockSpec((tm, LANES), lambda m: (m, 0)),
                     pl.BlockSpec((SUBLANES, LANES), lambda m: (m, 0))]
        out_shape = [jax.ShapeDtypeStruct((n, d), BF16), jax.ShapeDtypeStruct((n, LANES), F32),
                     jax.ShapeDtypeStruct((n // tm * SUBLANES, LANES), F32)]
    else:
        out_specs = [pl.BlockSpec((tm, d), lambda m: (m, 0))]
        out_shape = [jax.ShapeDtypeStruct((n, d), BF16)]
        if mode == "decay":
            out_specs += [pl.BlockSpec((tm, side), lambda m: (m, 0))]
            out_shape += [jax.ShapeDtypeStruct((n, side), F32)]
    return pl.pallas_call(
        functools.partial(_norm_kernel, mode=mode, n_experts=n_experts),
        grid=(n // tm,),
        in_specs=in_specs,
        out_specs=out_specs,
        out_shape=out_shape,
        compiler_params=_params("arbitrary"),
        name="norm_" + mode,
    )(h, norm_w.reshape(1, d), mod, mod, *extra)


def _mm_kernel(*refs, mode, n_w):
    x_ref = refs[0]
    w_refs = refs[1:1 + n_w]
    wb_refs = refs[len(refs) - n_w:]
    rest = refs[1 + n_w:len(refs) - n_w]

    @pl.when(pl.program_id(1) == 0)
    def _():
        for w_ref, wb_ref in zip(w_refs, wb_refs):
            wb_ref[...] = w_ref[...].astype(BF16)

    x = x_ref[...]
    if mode == "swiglu":
        (o_ref,) = rest
        o_ref[...] = (_silu(_dot(x, wb_refs[0][...])) * _dot(x, wb_refs[1][...])).astype(o_ref.dtype)
    elif mode == "resid":
        h_ref, g_ref, o_ref = rest
        o_ref[...] = h_ref[...] + g_ref[...] * _dot(x, wb_refs[0][...])
    else:
        (o_ref,) = rest
        o_ref[...] = _dot(x, wb_refs[0][...]).astype(o_ref.dtype)


def _mm_tiles(m_cap, k, n_cols, n_w, mode):
    tn = _tile(n_cols, 512 if (n_w > 1 or k > 4096) else 1024)
    tm = _tile(m_cap, 512 if (mode == "resid" or k > 4096) else 1024)
    io_bytes = {"plain": 2, "swiglu": 2, "resid": 8}[mode]
    est = n_w * k * tn * (4 + 2) + 2 * tm * k * 2 + 2 * tm * tn * io_bytes + tm * tn * 4 * n_w
    assert est <= VMEM_LIMIT_BYTES, (est, tm, tn)
    return tm, tn


def matmul(x, ws, layer, *, mode, out_dtype, m_cap, n_cols=None, h=None, mod=None, gate_col=None, row_group=None):
    m_tot, k = x.shape
    n_cols = ws[0].shape[2] if n_cols is None else n_cols
    tm, tn = _mm_tiles(m_cap, k, n_cols, len(ws), mode)
    w_spec = pl.BlockSpec((None, k, tn), lambda n, m: (layer, 0, n), pipeline_mode=pl.Buffered(1))
    in_specs = [pl.BlockSpec((tm, k), lambda n, m: (m, 0))] + [w_spec] * len(ws)
    args = [x, *ws]
    if mode == "resid":
        gate_blk = gate_col // tn
        in_specs += [
            pl.BlockSpec((tm, tn), lambda n, m: (m, n)),
            pl.BlockSpec((None, 1, tn), lambda n, m: (row_group(m, tm), 0, gate_blk + n)),
        ]
        args += [h, mod]
    return pl.pallas_call(
        functools.partial(_mm_kernel, mode=mode, n_w=len(ws)),
        grid=(n_cols // tn, m_tot // tm),
        in_specs=in_specs,
        out_specs=pl.BlockSpec((tm, tn), lambda n, m: (m, n)),
        out_shape=jax.ShapeDtypeStruct((m_tot, n_cols), out_dtype),
        scratch_shapes=[pltpu.VMEM((k, tn), BF16)] * len(ws),
        compiler_params=_params("arbitrary", "arbitrary"),
        name="mm_" + mode,
    )(*args)


def _cast_kernel(x_ref, o_ref):
    o_ref[...] = x_ref[...].astype(o_ref.dtype)


def cast_bf16(w, layer):
    _, n_e, k, f = w.shape
    tk = _tile(k, max(SUBLANES, 2 * 1024 * 1024 // f))
    return pl.pallas_call(
        _cast_kernel,
        grid=(n_e, k // tk),
        in_specs=[pl.BlockSpec((None, None, tk, f), lambda e, i: (layer, e, i, 0))],
        out_specs=pl.BlockSpec((None, tk, f), lambda e, i: (e, i, 0)),
        out_shape=jax.ShapeDtypeStruct(w.shape[1:], BF16),
        compiler_params=_params("arbitrary", "arbitrary"),
        name="cast_bf16",
    )(w)


SEG_ALIGN = 16


def _run_copies(t, n_experts, lo_ref, g_ref, len_ref, make_copy):
    def chunk_copy(e, c):
        base = t * n_experts + e
        lo = pl.multiple_of(lo_ref[base] + c * SEG_ALIGN, SEG_ALIGN)
        g = pl.multiple_of(g_ref[base] + c * SEG_ALIGN, SEG_ALIGN)
        return make_copy(lo, g)

    for wait in (False, True):
        for e in range(n_experts):
            def body(c, carry, e=e, wait=wait):
                cp = chunk_copy(e, c)
                if wait:
                    cp.wait()
                else:
                    cp.start()
                return carry

            lax.fori_loop(0, len_ref[t * n_experts + e] // SEG_ALIGN, body, 0)


def _dispatch_kernel(lo_ref, g_ref, len_ref, u_ref, meta_ref, *rest, n_experts):
    xs_hbm, sorted_ref, sem = rest[-3:]
    meta = meta_ref[...]
    lane = lax.broadcasted_iota(jnp.int32, meta.shape, 1)
    d1 = jnp.sum(jnp.where(lane == 6, meta, 0.0), axis=-1, keepdims=True).astype(jnp.int32)
    d2 = jnp.sum(jnp.where(lane == 7, meta, 0.0), axis=-1, keepdims=True).astype(jnp.int32)
    slot = lax.broadcasted_iota(jnp.int32, (meta.shape[0], sorted_ref.shape[0]), 1)
    onehot = jnp.logical_or(slot == d1, slot == d2).astype(BF16)
    sorted_ref[...] = _dot_tn(onehot, u_ref[...]).astype(BF16)

    def make_copy(lo, g):
        return pltpu.make_async_copy(sorted_ref.at[pl.ds(lo, SEG_ALIGN)], xs_hbm.at[pl.ds(g, SEG_ALIGN)], sem)

    _run_copies(pl.program_id(0), n_experts, lo_ref, g_ref, len_ref, make_copy)


def dispatch_rows(u, meta, seg_lo, seg_g, seg_len, p_rows, tm, n_experts, r_loc):
    n, d = u.shape
    return pl.pallas_call(
        functools.partial(_dispatch_kernel, n_experts=n_experts),
        grid_spec=pltpu.PrefetchScalarGridSpec(
            num_scalar_prefetch=3,
            grid=(n // tm,),
            in_specs=[
                pl.BlockSpec((tm, d), lambda t, lo, g, ln: (t, 0)),
                pl.BlockSpec((tm, LANES), lambda t, lo, g, ln: (t, 0)),
                pl.BlockSpec(memory_space=pl.ANY),
            ],
            out_specs=pl.BlockSpec(memory_space=pl.ANY),
            scratch_shapes=[pltpu.VMEM((r_loc, d), BF16), pltpu.SemaphoreType.DMA(())],
        ),
        out_shape=jax.ShapeDtypeStruct((p_rows, d), BF16),
        input_output_aliases={5: 0},
        compiler_params=_params("arbitrary"),
        name="moe_dispatch",
    )(seg_lo, seg_g, seg_len, u, meta, jnp.zeros((p_rows, d), BF16))


def _expert_swiglu_kernel(te_ref, na_ref, xs_ref, w1_ref, w3_ref, o_ref):
    active = pl.program_id(0) < na_ref[0]

    @pl.when(active)
    def _():
        x = xs_ref[...]
        o_ref[...] = (_silu(_dot(x, w1_ref[...])) * _dot(x, w3_ref[...])).astype(o_ref.dtype)

    @pl.when(jnp.logical_not(active))
    def _():
        o_ref[...] = jnp.zeros_like(o_ref)


def _expert_down_kernel(te_ref, na_ref, x_ref, w_ref, o_ref):
    active = pl.program_id(0) < na_ref[0]

    @pl.when(active)
    def _():
        o_ref[...] = _dot(x_ref[...], w_ref[...]).astype(o_ref.dtype)

    @pl.when(jnp.logical_not(active))
    def _():
        o_ref[...] = jnp.zeros_like(o_ref)


def expert_ffn(xs, w1, w3, w2, tile_expert, n_active, tm):
    p_rows, d = xs.shape
    f = w1.shape[2]
    tn = _tile(f, 512)
    nf = f // tn

    def w13_map(i, j, te, na):
        return te[i], 0, jnp.where(i < na[0], j, nf - 1)

    hid = pl.pallas_call(
        _expert_swiglu_kernel,
        grid_spec=pltpu.PrefetchScalarGridSpec(
            num_scalar_prefetch=2,
            grid=(p_rows // tm, nf),
            in_specs=[
                pl.BlockSpec((tm, d), lambda i, j, te, na: (i, 0)),
                pl.BlockSpec((None, d, tn), w13_map),
                pl.BlockSpec((None, d, tn), w13_map),
            ],
            out_specs=pl.BlockSpec((tm, tn), lambda i, j, te, na: (i, j)),
        ),
        out_shape=jax.ShapeDtypeStruct((p_rows, f), BF16),
        compiler_params=_params("arbitrary", "arbitrary"),
        name="expert_swiglu",
    )(tile_expert, n_active, xs, w1, w3)

    tn2 = _tile(d, 1024)
    nd = d // tn2

    def w2_map(i, j, te, na):
        return te[i], 0, jnp.where(i < na[0], j, nd - 1)

    return pl.pallas_call(
        _expert_down_kernel,
        grid_spec=pltpu.PrefetchScalarGridSpec(
            num_scalar_prefetch=2,
            grid=(p_rows // tm, nd),
            in_specs=[
                pl.BlockSpec((tm, f), lambda i, j, te, na: (i, 0)),
                pl.BlockSpec((None, f, tn2), w2_map),
            ],
            out_specs=pl.BlockSpec((tm, tn2), lambda i, j, te, na: (i, j)),
        ),
        out_shape=jax.ShapeDtypeStruct((p_rows, d), BF16),
        compiler_params=_params("arbitrary", "arbitrary"),
        name="expert_down",
    )(tile_expert, n_active, hid, w2)


def _combine_kernel(lo_ref, g_ref, len_ref, ys_hbm, meta_ref, h_ref, gate_ref, o_ref, runs_ref, sem, *, n_experts):
    @pl.when(pl.program_id(0) == 0)
    def _():
        runs_ref[...] = jnp.zeros_like(runs_ref)

    def make_copy(lo, g):
        return pltpu.make_async_copy(ys_hbm.at[pl.ds(g, SEG_ALIGN)], runs_ref.at[pl.ds(lo, SEG_ALIGN)], sem)

    _run_copies(pl.program_id(0), n_experts, lo_ref, g_ref, len_ref, make_copy)

    meta = meta_ref[...]
    lane = lax.broadcasted_iota(jnp.int32, meta.shape, 1)

    def field(k):
        return jnp.sum(jnp.where(lane == k, meta, 0.0), axis=-1, keepdims=True)

    slot = lax.broadcasted_iota(jnp.int32, (meta.shape[0], runs_ref.shape[0]), 1)
    runs = runs_ref[...]
    y1 = _dot((slot == field(6).astype(jnp.int32)).astype(BF16), runs)
    y2 = _dot((slot == field(7).astype(jnp.int32)).astype(BF16), runs)
    o_ref[...] = h_ref[...] + gate_ref[...] * (field(2) * y1 + field(3) * y2)


def combine_experts(ys, meta, seg_lo, seg_g, seg_len, h, mod, gate_col, row_group, tm, n_experts, r_loc):
    n, d = h.shape
    return pl.pallas_call(
        functools.partial(_combine_kernel, n_experts=n_experts),
        grid_spec=pltpu.PrefetchScalarGridSpec(
            num_scalar_prefetch=3,
            grid=(n // tm,),
            in_specs=[
                pl.BlockSpec(memory_space=pl.ANY),
                pl.BlockSpec((tm, LANES), lambda t, lo, g, ln: (t, 0)),
                pl.BlockSpec((tm, d), lambda t, lo, g, ln: (t, 0)),
                pl.BlockSpec((None, 1, d), lambda t, lo, g, ln: (row_group(t, tm), 0, gate_col // d)),
            ],
            out_specs=pl.BlockSpec((tm, d), lambda t, lo, g, ln: (t, 0)),
            scratch_shapes=[pltpu.VMEM((r_loc, d), BF16), pltpu.SemaphoreType.DMA(())],
        ),
        out_shape=jax.ShapeDtypeStruct((n, d), F32),
        compiler_params=_params("arbitrary"),
        name="moe_combine",
    )(seg_lo, seg_g, seg_len, ys, meta, h, mod)


def moe_ffn(u, meta, counts, h, mod, gate_col, row_group, w1, w3, w2, tm, tm_e):
    n, d = h.shape
    n_e = w1.shape[0]
    n_tt = n // tm
    i32 = jnp.int32
    experts = meta[:, 0:2].astype(i32)
    ranks = meta[:, 4:6].astype(i32)
    cnt = counts.reshape(n_tt, SUBLANES, LANES)[:, 0, :n_e].astype(i32)
    run_len = (cnt + SEG_ALIGN - 1) // SEG_ALIGN * SEG_ALIGN
    run_lo = jnp.cumsum(run_len, axis=1) - run_len
    region = (jnp.sum(run_len, axis=0) + tm_e - 1) // tm_e * tm_e
    region_end = jnp.cumsum(region)
    run_g = (region_end - region)[None, :] + jnp.cumsum(run_len, axis=0) - run_len
    onehot = experts[:, :, None] == jnp.arange(n_e)[None, None, :]
    lo_tok = jnp.repeat(run_lo, tm, axis=0)[:, None, :]
    dest = jnp.sum(jnp.where(onehot, lo_tok, 0), axis=-1) + ranks
    meta = lax.dynamic_update_slice(meta, dest.astype(F32), (0, 6))

    r_loc = 2 * tm + n_e * SEG_ALIGN
    p_rows = -(-(2 * n + n_tt * n_e * SEG_ALIGN + n_e * tm_e) // tm_e) * tm_e
    n_tiles = p_rows // tm_e
    n_active = (region_end[-1] // tm_e).reshape(1).astype(i32)
    tile_row = jnp.arange(n_tiles) * tm_e
    tile_expert = jnp.sum(tile_row[:, None] >= region_end[None, :], axis=1)
    last = jnp.sum(tile_expert * (jnp.arange(n_tiles) == n_active[0] - 1))
    tile_expert = jnp.where(jnp.arange(n_tiles) < n_active[0], tile_expert, last).astype(i32)

    seg = (run_lo.reshape(-1).astype(i32), run_g.reshape(-1).astype(i32), run_len.reshape(-1).astype(i32))
    xs = dispatch_rows(u, meta, *seg, p_rows, tm, n_e, r_loc)
    ys = expert_ffn(xs, w1, w3, w2, tile_expert, n_active, tm_e)
    return combine_experts(ys, meta, *seg, h, mod, gate_col, row_group, tm, n_e, r_loc)


def _qkprep_kernel(z_ref, qw_ref, kw_ref, cos_ref, sin_ref, qk_ref, kn_ref, *, n_q, n_kv, hd, scale):
    cos = cos_ref[...]
    sin = sin_ref[...]
    lane = lax.broadcasted_iota(jnp.int32, cos.shape, 1)
    quarter = hd // 4
    first = (lane % (2 * quarter)) < quarter
    for hh in range(n_q + n_kv):
        cols = slice(hh * hd, (hh + 1) * hd)
        x = z_ref[:, cols].astype(F32)
        ms = jnp.mean(x * x, axis=-1, keepdims=True)
        y = x * lax.rsqrt(ms + EPS) * (qw_ref[...] if hh < n_q else kw_ref[...])
        if hh >= n_q:
            kn_ref[:, (hh - n_q) * hd:(hh - n_q + 1) * hd] = y
        partner = jnp.where(first, pltpu.roll(y, hd - quarter, 1), pltpu.roll(y, quarter, 1))
        r = y * cos + partner * sin
        if hh < n_q:
            r = r * scale
        qk_ref[:, cols] = r.astype(BF16)


def qk_prepare(z, q_norm_w, k_norm_w, cos, sin, table_block, tm, *, n_q, n_kv, hd):
    n = z.shape[0]
    w = (n_q + n_kv) * hd
    return pl.pallas_call(
        functools.partial(_qkprep_kernel, n_q=n_q, n_kv=n_kv, hd=hd, scale=hd ** -0.5),
        grid=(n // tm,),
        in_specs=[
            pl.BlockSpec((tm, w), lambda m: (m, 0)),
            pl.BlockSpec((1, hd), lambda m: (0, 0)),
            pl.BlockSpec((1, hd), lambda m: (0, 0)),
            pl.BlockSpec((tm, hd), lambda m: (table_block(m, tm), 0)),
            pl.BlockSpec((tm, hd), lambda m: (table_block(m, tm), 0)),
        ],
        out_specs=[pl.BlockSpec((tm, w), lambda m: (m, 0)), pl.BlockSpec((tm, n_kv * hd), lambda m: (m, 0))],
        out_shape=[jax.ShapeDtypeStruct((n, w), BF16), jax.ShapeDtypeStruct((n, n_kv * hd), F32)],
        compiler_params=_params("arbitrary"),
        name="qk_prep",
    )(z, q_norm_w.reshape(1, hd), k_norm_w.reshape(1, hd), cos, sin)


def _attn_kernel(q_ref, k_ref, v_ref, *rest, group, hd, has_ctx):
    o_ref = rest[-1]
    if has_ctx:
        kc_ref, vc_ref = rest[:2]
        kc = kc_ref[...].astype(BF16)
        vc = vc_ref[...].astype(BF16)
    k = k_ref[...]
    v = v_ref[...]
    for gi in range(group):
        cols = slice(gi * hd, (gi + 1) * hd)
        q = q_ref[:, cols]
        s = _dot_nt(q, k)
        mx = jnp.max(s, axis=-1, keepdims=True)
        if has_ctx:
            sc = _dot_nt(q, kc)
            mx = jnp.maximum(mx, jnp.max(sc, axis=-1, keepdims=True))
        p = jnp.exp(s - mx)
        den = jnp.sum(p, axis=-1, keepdims=True)
        o = _dot(p.astype(BF16), v)
        if has_ctx:
            pc = jnp.exp(sc - mx)
            den = den + jnp.sum(pc, axis=-1, keepdims=True)
            o = o + _dot(pc.astype(BF16), vc)
        o_ref[:, cols] = (o * (1.0 / den)).astype(o_ref.dtype)


def _mix_target(mix, mix_shape, args, in_specs):
    if mix is None:
        return jax.ShapeDtypeStruct(mix_shape, BF16), {}
    args.append(mix)
    in_specs.append(pl.BlockSpec(memory_space=pl.ANY))
    return jax.ShapeDtypeStruct(mix.shape, mix.dtype), {len(args) - 1: 0}


def attention(qk, z, row0, n_batch, seq, *, n_q, n_kv, hd, v_col0, mix, mix_shape, ctx=None):
    group = n_q // n_kv
    tq = _tile(seq, 512)
    nq = seq // tq
    rb_q = row0 // tq
    rb_s = row0 // seq
    in_specs = [
        pl.BlockSpec((tq, group * hd), lambda b, g, i: (rb_q + b * nq + i, g)),
        pl.BlockSpec((seq, hd), lambda b, g, i: (rb_s + b, n_q + g)),
        pl.BlockSpec((seq, hd), lambda b, g, i: (rb_s + b, v_col0 // hd + g)),
    ]
    args = [qk, qk, z]
    if ctx is not None:
        cache_k, cache_v, layer = ctx
        past = cache_k.shape[2]
        c_spec = pl.BlockSpec((None, None, past, hd), lambda b, g, i: (b, layer, 0, g))
        in_specs += [c_spec, c_spec]
        args += [cache_k.reshape(*cache_k.shape[:3], n_kv * hd), cache_v.reshape(*cache_v.shape[:3], n_kv * hd)]
    out_shape, aliases = _mix_target(mix, mix_shape, args, in_specs)
    return pl.pallas_call(
        functools.partial(_attn_kernel, group=group, hd=hd, has_ctx=ctx is not None),
        grid=(n_batch, n_kv, nq),
        in_specs=in_specs,
        out_specs=pl.BlockSpec((tq, group * hd), lambda b, g, i: (rb_q + b * nq + i, g)),
        out_shape=out_shape,
        input_output_aliases=aliases,
        compiler_params=_params("arbitrary", "arbitrary", "arbitrary"),
        name="attn_ctx" if ctx is not None else "attn",
    )(*args)


def _gla_kernel(q_ref, k_ref, v_ref, g_ref, cf_ref, cb_ref, nw_ref, *rest, seq, chunk, has_s0, want_state, scale,
                n_aliased):
    rest = list(rest)
    s0_ref = rest.pop(0) if has_s0 else None
    del rest[:n_aliased]
    o_ref = rest.pop(0)
    sout_ref = rest.pop(0) if want_state else None
    qdf_ref, qdb_ref, oin_ref, kvf_ref, kvb_ref, spf_ref, spb_ref = rest
    n = seq // chunk
    unroll = True if n <= 4 else 4

    row = lax.broadcasted_iota(jnp.int32, (chunk, chunk), 0)
    col = lax.broadcasted_iota(jnp.int32, (chunk, chunk), 1)
    lower = col <= row
    upper = col >= row

    def rows(i):
        return pl.ds(pl.multiple_of(i * chunk, chunk), chunk)

    def intra(i, cum_ref, mask, tot_row, qd_ref, kv_ref):
        sl = rows(i)
        cum = cum_ref[sl, :]
        tot = cum[tot_row:tot_row + 1, :]
        q = q_ref[sl, :].astype(F32) * scale
        k = k_ref[sl, :].astype(F32)
        v = v_ref[sl, :]
        q_dec = (q * jnp.exp(cum)).astype(BF16)
        k_inv = (k * jnp.exp(-cum)).astype(BF16)
        k_end = (k * jnp.exp(tot - cum)).astype(BF16)
        qd_ref[sl, :] = q_dec
        kv_ref[i] = _dot_tn(v, k_end)
        att = jnp.where(mask, _dot_nt(q_dec, k_inv), 0.0).astype(BF16)
        return _dot(att, v)

    def phase1(i, carry):
        oin_ref[rows(i), :] = (intra(i, cf_ref, lower, chunk - 1, qdf_ref, kvf_ref)
                               + intra(i, cb_ref, upper, 0, qdb_ref, kvb_ref))
        return carry

    lax.fori_loop(0, n, phase1, 0, unroll=unroll)

    def scan(reverse, cum_ref, tot_row, kv_ref, sp_ref, init):
        def step(j, st):
            i = n - 1 - j if reverse else j
            tot = cum_ref[pl.ds(i * chunk + tot_row, 1), :]
            sp_ref[i] = st.astype(BF16)
            return jnp.exp(tot) * st + kv_ref[i]

        return lax.fori_loop(0, n, step, init)

    zero = jnp.zeros(kvf_ref.shape[1:], F32)
    st_f = scan(False, cf_ref, chunk - 1, kvf_ref, spf_ref, s0_ref[0].T if has_s0 else zero)
    st_b = scan(True, cb_ref, 0, kvb_ref, spb_ref, s0_ref[1].T if has_s0 else zero)
    if want_state:
        sout_ref[0] = st_f.T
        sout_ref[1] = st_b.T

    def phase3(i, carry):
        sl = rows(i)
        o = oin_ref[sl, :] + _dot_nt(qdf_ref[sl, :], spf_ref[i]) + _dot_nt(qdb_ref[sl, :], spb_ref[i])
        ms = jnp.mean(o * o, axis=-1, keepdims=True)
        y = o * lax.rsqrt(ms + EPS) * nw_ref[...]
        o_ref[sl, :] = (y * _silu(g_ref[sl, :].astype(F32))).astype(o_ref.dtype)
        return carry

    lax.fori_loop(0, n, phase3, 0, unroll=unroll)


def gla(z, la, norm_w, row0, n_batch, seq, mix, mix_col0, *, heads, dk, dv, q_col0, k_col0, v_col0, g_col0, s0=None,
        new_state=None):
    rb = row0 // seq
    n_chunks = seq // GLA_CHUNK
    want_state = new_state is not None
    in_specs = [
        pl.BlockSpec((seq, dk), lambda b, hh: (rb + b, q_col0 // dk + hh)),
        pl.BlockSpec((seq, dk), lambda b, hh: (rb + b, k_col0 // dk + hh)),
        pl.BlockSpec((seq, dv), lambda b, hh: (rb + b, v_col0 // dv + hh)),
        pl.BlockSpec((seq, dv), lambda b, hh: (rb + b, g_col0 // dv + hh)),
        pl.BlockSpec((seq, dk), lambda b, hh: (rb + b, hh)),
        pl.BlockSpec((seq, dk), lambda b, hh: (rb + b, heads + hh)),
        pl.BlockSpec((1, dv), lambda b, hh: (0, 0)),
    ]
    args = [z, z, z, z, la, la, norm_w.reshape(1, dv)]
    if s0 is not None:
        state, layer = s0
        in_specs += [pl.BlockSpec((None, None, 2, None, dk, dv), lambda b, hh: (b, layer, 0, hh, 0, 0))]
        args += [state]
    n_fixed = len(args)
    mix_shape, aliases = _mix_target(mix, None, args, in_specs)
    out_specs = [pl.BlockSpec((seq, dv), lambda b, hh: (rb + b, mix_col0 // dv + hh))]
    out_shape = [mix_shape]
    if want_state:
        states, layer_out, depth = new_state
        out_specs += [pl.BlockSpec((None, None, 2, None, dk, dv), lambda b, hh: (b, layer_out, 0, hh, 0, 0))]
        out_shape += [jax.ShapeDtypeStruct((n_batch, depth, 2, heads, dk, dv), F32)]
        if states is not None:
            args.append(states)
            in_specs.append(pl.BlockSpec(memory_space=pl.ANY))
            aliases[len(args) - 1] = 1
    res = pl.pallas_call(
        functools.partial(_gla_kernel, seq=seq, chunk=GLA_CHUNK, has_s0=s0 is not None, want_state=want_state,
                          scale=dk ** -0.5, n_aliased=len(args) - n_fixed),
        grid=(n_batch, heads),
        in_specs=in_specs,
        out_specs=out_specs,
        out_shape=out_shape,
        input_output_aliases=aliases,
        scratch_shapes=[pltpu.VMEM((seq, dk), BF16), pltpu.VMEM((seq, dk), BF16), pltpu.VMEM((seq, dv), F32),
                        pltpu.VMEM((n_chunks, dv, dk), F32), pltpu.VMEM((n_chunks, dv, dk), F32),
                        pltpu.VMEM((n_chunks, dv, dk), BF16), pltpu.VMEM((n_chunks, dv, dk), BF16)],
        compiler_params=_params("arbitrary", "arbitrary"),
        name="gla_state" if want_state else "gla",
    )(*args)
    return res if want_state else (res[0], None)


def _rope_tables(seq, hd, lead):
    axis_dim = hd // 2
    t = jnp.arange(seq)
    rowp = (t // GRID_W).astype(F32)
    colp = (t % GRID_W).astype(F32)
    inv = ROPE_THETA ** (-jnp.arange(axis_dim // 2, dtype=F32) * 2.0 / axis_dim)
    ang_r = rowp[:, None] * inv
    ang_c = colp[:, None] * inv
    cos = jnp.concatenate([jnp.cos(ang_r), jnp.cos(ang_r), jnp.cos(ang_c), jnp.cos(ang_c)], axis=-1)
    sin = jnp.concatenate([-jnp.sin(ang_r), jnp.sin(ang_r), -jnp.sin(ang_c), jnp.sin(ang_c)], axis=-1)
    cos = jnp.concatenate([jnp.ones((lead, hd), F32), cos], axis=0)
    sin = jnp.concatenate([jnp.zeros((lead, hd), F32), sin], axis=0)
    return cos, sin


def kernel(x_prompt, x_sample, cache_k, cache_v, state_gla, c, c_ctx, norm1_w, norm2_w, w_ada, b_ada, w_in,
           q_norm_w, k_norm_w, gla_up, gla_bias, gla_norm_w, w_out, ffn_w1, ffn_w3, ffn_w2, router_w, moe_w1,
           moe_w3, moe_w2):
    batch, seq1, d = x_prompt.shape
    dec_batch, seq2, _ = x_sample.shape
    depth = w_in.shape[0]
    hd = q_norm_w.shape[-1]
    n_kv = cache_k.shape[3]
    n_q = d // 2 // hd
    heads, dk, dv = state_gla.shape[3:]
    rank = gla_up.shape[2]
    n_experts = router_w.shape[-1]
    attn_w, kv_w, key_w, gla_w = n_q * hd, n_kv * hd, heads * dk, heads * dv
    main_cols = attn_w + 2 * kv_w + 2 * key_w + 2 * gla_w
    q_col0 = attn_w + 2 * kv_w
    k_col0 = q_col0 + key_w
    v_col0 = k_col0 + key_w
    g_col0 = v_col0 + gla_w
    n1, n2 = batch * seq1, dec_batch * seq2
    n = n1 + n2
    assert hd == LANES and n1 % seq2 == 0 and 2 * rank <= LANES and n_experts <= LANES

    def row_group(m, tm):
        tok = m * tm
        return jnp.where(tok < n1, 0, 1 + (tok - n1) // seq2)

    def table_block(m, tm):
        tok = m * tm
        return jnp.where(tok < n1, 0, 1 + ((tok - n1) % seq2) // tm)

    n_rows = -(-(1 + dec_batch) // SUBLANES) * SUBLANES
    cond = jnp.zeros((n_rows, d), F32).at[0].set(c_ctx).at[1:1 + dec_batch].set(c)
    mod_all = ada_modulation(cond, w_ada, b_ada)

    tm_row = _tile(math.gcd(n1, seq2), 256)
    tm_prep = _tile(math.gcd(n1, seq2), 512)
    m_cap = math.gcd(n1, seq2)
    cos, sin = _rope_tables(seq2, hd, tm_prep)

    h = jnp.concatenate([x_prompt.reshape(n1, d), x_sample.reshape(n2, d)], axis=0)
    new_k, new_v, new_s = [], [], None
    for l in range(depth):
        mod = mod_all[l].reshape(n_rows, 1, N_MOD * d)
        w_lr = jnp.zeros((d, LANES), BF16).at[:, :2 * rank].set(w_in[l, :, main_cols:].astype(BF16))
        up = jnp.zeros((LANES, 2 * key_w), F32)
        up = up.at[:rank, :key_w].set(gla_up[l, 0]).at[rank:2 * rank, key_w:].set(gla_up[l, 1])
        bias = gla_bias[l].reshape(1, 2 * key_w)

        u, la = norm_modulate(h, norm1_w[l], mod, 1, 0, row_group, tm_row, mode="decay", extra=(w_lr, up, bias))
        z = matmul(u, [w_in], l, mode="plain", out_dtype=BF16, m_cap=m_cap, n_cols=main_cols)

        qk, kn = qk_prepare(z, q_norm_w[l], k_norm_w[l], cos, sin, table_block, tm_prep, n_q=n_q, n_kv=n_kv, hd=hd)
        new_k.append(kn[:n1].reshape(batch, seq1, n_kv, hd))
        new_v.append(z[:n1, attn_w + kv_w:attn_w + 2 * kv_w].astype(F32).reshape(batch, seq1, n_kv, hd))

        attn_kw = dict(n_q=n_q, n_kv=n_kv, hd=hd, v_col0=attn_w + kv_w, mix_shape=(n, attn_w + gla_w))
        mix = attention(qk, z, 0, batch, seq1, mix=None, **attn_kw)
        mix = attention(qk, z, n1, dec_batch, seq2, mix=mix, ctx=(cache_k, cache_v, l), **attn_kw)
        gla_kw = dict(heads=heads, dk=dk, dv=dv, q_col0=q_col0, k_col0=k_col0, v_col0=v_col0, g_col0=g_col0)
        mix, new_s = gla(z, la, gla_norm_w[l], 0, batch, seq1, mix, attn_w, new_state=(new_s, l, depth), **gla_kw)
        mix, _ = gla(z, la, gla_norm_w[l], n1, dec_batch, seq2, mix, attn_w, s0=(state_gla, l), **gla_kw)

        h = matmul(mix, [w_out], l, mode="resid", out_dtype=F32, m_cap=m_cap, h=h, mod=mod, gate_col=2 * d,
                   row_group=row_group)

        i = l // 2
        if l % 2 == 0:
            (u2,) = norm_modulate(h, norm2_w[l], mod, 4, 3, row_group, tm_row, mode="plain", extra=())
            hid = matmul(u2, [ffn_w1, ffn_w3], i, mode="swiglu", out_dtype=BF16, m_cap=m_cap)
            h = matmul(hid, [ffn_w2], i, mode="resid", out_dtype=F32, m_cap=m_cap, h=h, mod=mod, gate_col=5 * d,
                       row_group=row_group)
        else:
            rw = jnp.zeros((d, LANES), F32).at[:, :n_experts].set(router_w[i])
            xp, meta, counts = norm_modulate(h, norm2_w[l], mod, 4, 3, row_group, tm_row, mode="router",
                                             extra=(rw,), n_experts=n_experts)
            h = moe_ffn(xp, meta, counts, h, mod, 5 * d, row_group, cast_bf16(moe_w1, i), cast_bf16(moe_w3, i),
                        cast_bf16(moe_w2, i), tm_row, tm_prep)

    y_prompt = h[:n1].reshape(batch, seq1, d)
    y_sample = h[n1:].reshape(dec_batch, seq2, d)
    return (y_prompt, y_sample, jnp.stack(new_k, axis=1), jnp.stack(new_v, axis=1),
            new_s.astype(x_prompt.dtype))
```

```python
import functools
import math

import jax
import jax.numpy as jnp
from jax import lax
from jax.experimental import pallas as pl
from jax.experimental.pallas import tpu as pltpu

F32 = jnp.float32
BF16 = jnp.bfloat16

EPS = 1e-6
GRID_W = 64
GLA_CHUNK = 64
GLA_TAU = 16.0
ROPE_THETA = 10000.0
N_MOD = 6
LANES = 128
SUBLANES = 8
VMEM_LIMIT_BYTES = 56 * 1024 * 1024


def _params(*semantics):
    return pltpu.CompilerParams(dimension_semantics=semantics, vmem_limit_bytes=VMEM_LIMIT_BYTES)


def _tile(n, cap):
    t = cap
    while t > 1 and n % t:
        t //= 2
    return t


def _dot(a, b):
    return jnp.dot(a, b, preferred_element_type=F32)


def _dot_nt(a, b):
    return lax.dot_general(a, b, (((1,), (1,)), ((), ())), preferred_element_type=F32)


def _dot_tn(a, b):
    return lax.dot_general(a, b, (((0,), (0,)), ((), ())), preferred_element_type=F32)


def _split_bf16(x):
    hi = x.astype(BF16)
    lo = (x - hi.astype(F32)).astype(BF16)
    return hi, lo


def _silu(x):
    return x * jax.nn.sigmoid(x)


def _ada_kernel(c_ref, w_ref, b_ref, o_ref):
    s = _silu(c_ref[...]).astype(BF16)
    o_ref[...] = _dot(s, w_ref[...].astype(BF16)) + b_ref[...]


def ada_modulation(cond, w_ada, b_ada):
    depth, d, n = w_ada.shape
    r = cond.shape[0]
    tn = _tile(n, 1024)
    return pl.pallas_call(
        _ada_kernel,
        grid=(depth, n // tn),
        in_specs=[
            pl.BlockSpec((r, d), lambda l, j: (0, 0)),
            pl.BlockSpec((None, d, tn), lambda l, j: (l, 0, j)),
            pl.BlockSpec((None, 1, tn), lambda l, j: (l, 0, j)),
        ],
        out_specs=pl.BlockSpec((None, r, tn), lambda l, j: (l, 0, j)),
        out_shape=jax.ShapeDtypeStruct((depth, r, n), F32),
        compiler_params=_params("arbitrary", "arbitrary"),
        name="ada",
    )(cond, w_ada, b_ada.reshape(depth, 1, n))


def _norm_kernel(h_ref, nw_ref, sc_ref, sh_ref, *rest, mode, n_experts):
    x = h_ref[...]
    ms = jnp.mean(x * x, axis=-1, keepdims=True)
    y = x * lax.rsqrt(ms + EPS) * nw_ref[...]
    u = y * (1.0 + sc_ref[...]) + sh_ref[...]
    if mode == "decay":
        wlr_ref, up_ref, bias_ref, u_ref, la_ref = rest
        ub = u.astype(BF16)
        u_ref[...] = ub
        lr = _dot(ub, wlr_ref[...])
        lr_hi, lr_lo = _split_bf16(lr)
        up_hi, up_lo = _split_bf16(up_ref[...])
        logits = _dot(lr_hi, up_hi) + (_dot(lr_lo, up_hi) + _dot(lr_hi, up_lo)) + bias_ref[...]
        log_sig = jnp.minimum(logits, 0.0) - jnp.log1p(jnp.exp(-jnp.abs(logits)))
        la = log_sig * (1.0 / GLA_TAU)
        tm = la.shape[0]
        half = la.shape[1] // 2
        t_row = lax.broadcasted_iota(jnp.int32, (tm, tm), 0)
        t_col = lax.broadcasted_iota(jnp.int32, (tm, tm), 1)
        same_chunk = (t_row // GLA_CHUNK) == (t_col // GLA_CHUNK)
        tri_f = jnp.logical_and(same_chunk, t_col <= t_row).astype(BF16)
        tri_b = jnp.logical_and(same_chunk, t_col >= t_row).astype(BF16)
        la_hi, la_lo = _split_bf16(la)
        la_ref[:, :half] = _dot(tri_f, la_hi[:, :half]) + _dot(tri_f, la_lo[:, :half])
        la_ref[:, half:] = _dot(tri_b, la_hi[:, half:]) + _dot(tri_b, la_lo[:, half:])
    elif mode == "plain":
        (u_ref,) = rest
        u_ref[...] = u.astype(BF16)
    else:
        rw_ref, u_ref, meta_ref, cnt_ref = rest
        u_ref[...] = u.astype(BF16)
        logits = jnp.dot(u, rw_ref[...], preferred_element_type=F32, precision=lax.Precision.HIGHEST)
        lane = lax.broadcasted_iota(jnp.int32, logits.shape, 1)
        neg = jnp.float32(-jnp.inf)
        logits = jnp.where(lane < n_experts, logits, neg)
        v1 = jnp.max(logits, axis=-1, keepdims=True)
        i1 = jnp.min(jnp.where(logits == v1, lane, LANES), axis=-1, keepdims=True)
        rest_l = jnp.where(lane == i1, neg, logits)
        v2 = jnp.max(rest_l, axis=-1, keepdims=True)
        i2 = jnp.min(jnp.where(rest_l == v2, lane, LANES), axis=-1, keepdims=True)
        e2 = jnp.exp(v2 - v1)
        g1 = 1.0 / (1.0 + e2)
        g2 = e2 / (1.0 + e2)

        tm = logits.shape[0]
        chosen = jnp.logical_or(lane == i1, lane == i2)
        t_row = lax.broadcasted_iota(jnp.int32, (tm, tm), 0)
        t_col = lax.broadcasted_iota(jnp.int32, (tm, tm), 1)
        before = _dot((t_col < t_row).astype(BF16), chosen.astype(BF16))
        r1 = jnp.sum(jnp.where(lane == i1, before, 0.0), axis=-1, keepdims=True)
        r2 = jnp.sum(jnp.where(lane == i2, before, 0.0), axis=-1, keepdims=True)
        cnt_ref[...] = jnp.broadcast_to(jnp.sum(chosen.astype(F32), axis=0, keepdims=True), cnt_ref.shape)
        fields = (i1.astype(F32), i2.astype(F32), g1, g2, r1, r2)
        meta = jnp.zeros(logits.shape, F32)
        for pos, val in enumerate(fields):
            meta = jnp.where(lane == pos, val, meta)
        meta_ref[...] = meta


def norm_modulate(h, norm_w, mod, sc_col, sh_col, row_group, tm, *, mode, extra, n_experts=0):
    n, d = h.shape
    in_specs = [
        pl.BlockSpec((tm, d), lambda m: (m, 0)),
        pl.BlockSpec((1, d), lambda m: (0, 0)),
        pl.BlockSpec((None, 1, d), lambda m: (row_group(m, tm), 0, sc_col)),
        pl.BlockSpec((None, 1, d), lambda m: (row_group(m, tm), 0, sh_col)),
    ]
    if mode == "decay":
        wlr, up, bias = extra
        in_specs += [
            pl.BlockSpec(wlr.shape, lambda m: (0, 0)),
            pl.BlockSpec(up.shape, lambda m: (0, 0)),
            pl.BlockSpec(bias.shape, lambda m: (0, 0)),
        ]
        side = up.shape[1]
    elif mode == "router":
        (rw,) = extra
        in_specs += [pl.BlockSpec(rw.shape, lambda m: (0, 0))]
    if mode == "router":
        out_specs = [pl.BlockSpec((tm, d), lambda m: (m, 0)),
                     pl.BlockSpec((tm, LANES), lambda m: (m, 0)),
                     pl.BlockSpec((SUBLANES, LANES), lambda m: (m, 0))]
        out_shape = [jax.ShapeDtypeStruct((n, d), BF16), jax.ShapeDtypeStruct((n, LANES), F32),
                     jax.ShapeDtypeStruct((n // tm * SUBLANES, LANES), F32)]
    else:
        out_specs = [pl.BlockSpec((tm, d), lambda m: (m, 0))]
        out_shape = [jax.ShapeDtypeStruct((n, d), BF16)]
        if mode == "decay":
            out_specs += [pl.BlockSpec((tm, side), lambda m: (m, 0))]
            out_shape += [jax.ShapeDtypeStruct((n, side), F32)]
    return pl.pallas_call(
        functools.partial(_norm_kernel, mode=mode, n_experts=n_experts),
        grid=(n // tm,),
        in_specs=in_specs,
        out_specs=out_specs,
        out_shape=out_shape,
        compiler_params=_params("arbitrary"),
        name="norm_" + mode,
    )(h, norm_w.reshape(1, d), mod, mod, *extra)


def _mm_kernel(*refs, mode, n_w, w_transposed):
    mm = _dot_nt if w_transposed else _dot
    x_ref = refs[0]
    w_refs = refs[1:1 + n_w]
    wb_refs = refs[len(refs) - n_w:]
    rest = refs[1 + n_w:len(refs) - n_w]

    @pl.when(pl.program_id(1) == 0)
    def _():
        for w_ref, wb_ref in zip(w_refs, wb_refs):
            wb_ref[...] = w_ref[...].astype(BF16)

    x = x_ref[...]
    if mode == "swiglu":
        (o_ref,) = rest
        o_ref[...] = (_silu(mm(x, wb_refs[0][...])) * mm(x, wb_refs[1][...])).astype(o_ref.dtype)
    elif mode == "resid":
        h_ref, g_ref, o_ref = rest
        o_ref[...] = h_ref[...] + g_ref[...] * mm(x, wb_refs[0][...])
    else:
        (o_ref,) = rest
        o_ref[...] = mm(x, wb_refs[0][...]).astype(o_ref.dtype)


def _mm_tiles(m_cap, k, n_cols, n_w, mode):
    tn = _tile(n_cols, 512 if (n_w > 1 or k > 4096) else 1024)
    tm = _tile(m_cap, 512 if (mode == "resid" or k > 4096) else 1024)
    io_bytes = {"plain": 2, "swiglu": 2, "resid": 8}[mode]
    est = n_w * k * tn * (4 + 2) + 2 * tm * k * 2 + 2 * tm * tn * io_bytes + tm * tn * 4 * n_w
    assert est <= VMEM_LIMIT_BYTES, (est, tm, tn)
    return tm, tn


def matmul(x, ws, layer, *, mode, out_dtype, m_cap, n_cols=None, w_transposed=False, h=None, mod=None,
           gate_col=None, row_group=None):
    m_tot, k = x.shape
    n_cols = ws[0].shape[1 if w_transposed else 2] if n_cols is None else n_cols
    tm, tn = _mm_tiles(m_cap, k, n_cols, len(ws), mode)
    if w_transposed:
        w_spec = pl.BlockSpec((None, tn, k), lambda n, m: (layer, n, 0), pipeline_mode=pl.Buffered(1))
    else:
        w_spec = pl.BlockSpec((None, k, tn), lambda n, m: (layer, 0, n), pipeline_mode=pl.Buffered(1))
    in_specs = [pl.BlockSpec((tm, k), lambda n, m: (m, 0))] + [w_spec] * len(ws)
    args = [x, *ws]
    if mode == "resid":
        gate_blk = gate_col // tn
        in_specs += [
            pl.BlockSpec((tm, tn), lambda n, m: (m, n)),
            pl.BlockSpec((None, 1, tn), lambda n, m: (row_group(m, tm), 0, gate_blk + n)),
        ]
        args += [h, mod]
    return pl.pallas_call(
        functools.partial(_mm_kernel, mode=mode, n_w=len(ws), w_transposed=w_transposed),
        grid=(n_cols // tn, m_tot // tm),
        in_specs=in_specs,
        out_specs=pl.BlockSpec((tm, tn), lambda n, m: (m, n)),
        out_shape=jax.ShapeDtypeStruct((m_tot, n_cols), out_dtype),
        scratch_shapes=[pltpu.VMEM((tn, k) if w_transposed else (k, tn), BF16)] * len(ws),
        compiler_params=_params("arbitrary", "arbitrary"),
        name="mm_" + mode,
    )(*args)


def _cast_kernel(x_ref, o_ref):
    o_ref[...] = x_ref[...].astype(o_ref.dtype)


def cast_bf16(w, layer):
    _, n_e, k, f = w.shape
    tk = _tile(k, max(SUBLANES, 2 * 1024 * 1024 // f))
    return pl.pallas_call(
        _cast_kernel,
        grid=(n_e, k // tk),
        in_specs=[pl.BlockSpec((None, None, tk, f), lambda e, i: (layer, e, i, 0))],
        out_specs=pl.BlockSpec((None, tk, f), lambda e, i: (e, i, 0)),
        out_shape=jax.ShapeDtypeStruct(w.shape[1:], BF16),
        compiler_params=_params("arbitrary", "arbitrary"),
        name="cast_bf16",
    )(w)


SEG_ALIGN = 16


def _run_copies(t, n_experts, lo_ref, g_ref, len_ref, make_copy, wait):
    for e in range(n_experts):
        base = t * n_experts + e

        def body(c, carry, base=base):
            lo = pl.multiple_of(lo_ref[base] + c * SEG_ALIGN, SEG_ALIGN)
            g = pl.multiple_of(g_ref[base] + c * SEG_ALIGN, SEG_ALIGN)
            cp = make_copy(lo, g)
            if wait:
                cp.wait()
            else:
                cp.start()
            return carry

        lax.fori_loop(0, len_ref[base] // SEG_ALIGN, body, 0)


def _dispatch_kernel(lo_ref, g_ref, len_ref, u_ref, meta_ref, *rest, n_experts):
    xs_hbm, sorted_ref, sem = rest[-3:]
    meta = meta_ref[...]
    lane = lax.broadcasted_iota(jnp.int32, meta.shape, 1)
    d1 = jnp.sum(jnp.where(lane == 6, meta, 0.0), axis=-1, keepdims=True).astype(jnp.int32)
    d2 = jnp.sum(jnp.where(lane == 7, meta, 0.0), axis=-1, keepdims=True).astype(jnp.int32)
    slot = lax.broadcasted_iota(jnp.int32, (meta.shape[0], sorted_ref.shape[0]), 1)
    onehot = jnp.logical_or(slot == d1, slot == d2).astype(BF16)
    sorted_ref[...] = _dot_tn(onehot, u_ref[...]).astype(BF16)

    def make_copy(lo, g):
        return pltpu.make_async_copy(sorted_ref.at[pl.ds(lo, SEG_ALIGN)], xs_hbm.at[pl.ds(g, SEG_ALIGN)], sem)

    for wait in (False, True):
        _run_copies(pl.program_id(0), n_experts, lo_ref, g_ref, len_ref, make_copy, wait)


def dispatch_rows(u, meta, seg_lo, seg_g, seg_len, p_rows, tm, n_experts, r_loc):
    n, d = u.shape
    return pl.pallas_call(
        functools.partial(_dispatch_kernel, n_experts=n_experts),
        grid_spec=pltpu.PrefetchScalarGridSpec(
            num_scalar_prefetch=3,
            grid=(n // tm,),
            in_specs=[
                pl.BlockSpec((tm, d), lambda t, lo, g, ln: (t, 0)),
                pl.BlockSpec((tm, LANES), lambda t, lo, g, ln: (t, 0)),
                pl.BlockSpec(memory_space=pl.ANY),
            ],
            out_specs=pl.BlockSpec(memory_space=pl.ANY),
            scratch_shapes=[pltpu.VMEM((r_loc, d), BF16), pltpu.SemaphoreType.DMA(())],
        ),
        out_shape=jax.ShapeDtypeStruct((p_rows, d), BF16),
        input_output_aliases={5: 0},
        compiler_params=_params("arbitrary"),
        name="moe_dispatch",
    )(seg_lo, seg_g, seg_len, u, meta, jnp.zeros((p_rows, d), BF16))


def _expert_swiglu_kernel(te_ref, na_ref, xs_ref, w1_ref, w3_ref, o_ref):
    active = pl.program_id(0) < na_ref[0]

    @pl.when(active)
    def _():
        x = xs_ref[...]
        o_ref[...] = (_silu(_dot(x, w1_ref[...])) * _dot(x, w3_ref[...])).astype(o_ref.dtype)

    @pl.when(jnp.logical_not(active))
    def _():
        o_ref[...] = jnp.zeros_like(o_ref)


def _expert_down_kernel(te_ref, na_ref, x_ref, w_ref, o_ref):
    active = pl.program_id(0) < na_ref[0]

    @pl.when(active)
    def _():
        o_ref[...] = _dot(x_ref[...], w_ref[...]).astype(o_ref.dtype)

    @pl.when(jnp.logical_not(active))
    def _():
        o_ref[...] = jnp.zeros_like(o_ref)


def expert_ffn(xs, w1, w3, w2, tile_expert, n_active, tm):
    p_rows, d = xs.shape
    f = w1.shape[2]
    tn = _tile(f, 512)
    nf = f // tn

    def w13_map(i, j, te, na):
        return te[i], 0, jnp.where(i < na[0], j, nf - 1)

    hid = pl.pallas_call(
        _expert_swiglu_kernel,
        grid_spec=pltpu.PrefetchScalarGridSpec(
            num_scalar_prefetch=2,
            grid=(p_rows // tm, nf),
            in_specs=[
                pl.BlockSpec((tm, d), lambda i, j, te, na: (i, 0)),
                pl.BlockSpec((None, d, tn), w13_map),
                pl.BlockSpec((None, d, tn), w13_map),
            ],
            out_specs=pl.BlockSpec((tm, tn), lambda i, j, te, na: (i, j)),
        ),
        out_shape=jax.ShapeDtypeStruct((p_rows, f), BF16),
        compiler_params=_params("arbitrary", "arbitrary"),
        name="expert_swiglu",
    )(tile_expert, n_active, xs, w1, w3)

    tn2 = _tile(d, 1024)
    nd = d // tn2

    def w2_map(i, j, te, na):
        return te[i], 0, jnp.where(i < na[0], j, nd - 1)

    return pl.pallas_call(
        _expert_down_kernel,
        grid_spec=pltpu.PrefetchScalarGridSpec(
            num_scalar_prefetch=2,
            grid=(p_rows // tm, nd),
            in_specs=[
                pl.BlockSpec((tm, f), lambda i, j, te, na: (i, 0)),
                pl.BlockSpec((None, f, tn2), w2_map),
            ],
            out_specs=pl.BlockSpec((tm, tn2), lambda i, j, te, na: (i, j)),
        ),
        out_shape=jax.ShapeDtypeStruct((p_rows, d), BF16),
        compiler_params=_params("arbitrary", "arbitrary"),
        name="expert_down",
    )(tile_expert, n_active, hid, w2)


def _combine_kernel(lo_ref, g_ref, len_ref, ys_hbm, meta_ref, h_ref, gate_ref, o_ref, runs_ref, sem, *, n_experts):
    t = pl.program_id(0)
    buf = t % 2

    def copies(tile, b, wait):
        def make_copy(lo, g):
            return pltpu.make_async_copy(ys_hbm.at[pl.ds(g, SEG_ALIGN)], runs_ref.at[b, pl.ds(lo, SEG_ALIGN)],
                                         sem.at[b])

        _run_copies(tile, n_experts, lo_ref, g_ref, len_ref, make_copy, wait)

    @pl.when(t == 0)
    def _():
        runs_ref[...] = jnp.zeros_like(runs_ref)
        copies(t, buf, False)

    @pl.when(t + 1 < pl.num_programs(0))
    def _():
        copies(t + 1, 1 - buf, False)

    copies(t, buf, True)

    meta = meta_ref[...]
    lane = lax.broadcasted_iota(jnp.int32, meta.shape, 1)

    def field(k):
        return jnp.sum(jnp.where(lane == k, meta, 0.0), axis=-1, keepdims=True)

    slot = lax.broadcasted_iota(jnp.int32, (meta.shape[0], runs_ref.shape[1]), 1)
    runs = runs_ref[buf]
    y1 = _dot((slot == field(6).astype(jnp.int32)).astype(BF16), runs)
    y2 = _dot((slot == field(7).astype(jnp.int32)).astype(BF16), runs)
    o_ref[...] = h_ref[...] + gate_ref[...] * (field(2) * y1 + field(3) * y2)


def combine_experts(ys, meta, seg_lo, seg_g, seg_len, h, mod, gate_col, row_group, tm, n_experts, r_loc):
    n, d = h.shape
    return pl.pallas_call(
        functools.partial(_combine_kernel, n_experts=n_experts),
        grid_spec=pltpu.PrefetchScalarGridSpec(
            num_scalar_prefetch=3,
            grid=(n // tm,),
            in_specs=[
                pl.BlockSpec(memory_space=pl.ANY),
                pl.BlockSpec((tm, LANES), lambda t, lo, g, ln: (t, 0)),
                pl.BlockSpec((tm, d), lambda t, lo, g, ln: (t, 0)),
                pl.BlockSpec((None, 1, d), lambda t, lo, g, ln: (row_group(t, tm), 0, gate_col // d)),
            ],
            out_specs=pl.BlockSpec((tm, d), lambda t, lo, g, ln: (t, 0)),
            scratch_shapes=[pltpu.VMEM((2, r_loc, d), BF16), pltpu.SemaphoreType.DMA((2,))],
        ),
        out_shape=jax.ShapeDtypeStruct((n, d), F32),
        compiler_params=_params("arbitrary"),
        name="moe_combine",
    )(seg_lo, seg_g, seg_len, ys, meta, h, mod)


def moe_ffn(u, meta, counts, h, mod, gate_col, row_group, w1, w3, w2, tm, tm_e):
    n, d = h.shape
    n_e = w1.shape[0]
    n_tt = n // tm
    i32 = jnp.int32
    experts = meta[:, 0:2].astype(i32)
    ranks = meta[:, 4:6].astype(i32)
    cnt = counts.reshape(n_tt, SUBLANES, LANES)[:, 0, :n_e].astype(i32)
    run_len = (cnt + SEG_ALIGN - 1) // SEG_ALIGN * SEG_ALIGN
    run_lo = jnp.cumsum(run_len, axis=1) - run_len
    region = (jnp.sum(run_len, axis=0) + tm_e - 1) // tm_e * tm_e
    region_end = jnp.cumsum(region)
    run_g = (region_end - region)[None, :] + jnp.cumsum(run_len, axis=0) - run_len
    onehot = experts[:, :, None] == jnp.arange(n_e)[None, None, :]
    lo_tok = jnp.repeat(run_lo, tm, axis=0)[:, None, :]
    dest = jnp.sum(jnp.where(onehot, lo_tok, 0), axis=-1) + ranks
    meta = lax.dynamic_update_slice(meta, dest.astype(F32), (0, 6))

    r_loc = 2 * tm + n_e * SEG_ALIGN
    p_rows = -(-(2 * n + n_tt * n_e * SEG_ALIGN + n_e * tm_e) // tm_e) * tm_e
    n_tiles = p_rows // tm_e
    n_active = (region_end[-1] // tm_e).reshape(1).astype(i32)
    tile_row = jnp.arange(n_tiles) * tm_e
    tile_expert = jnp.sum(tile_row[:, None] >= region_end[None, :], axis=1)
    last = jnp.sum(tile_expert * (jnp.arange(n_tiles) == n_active[0] - 1))
    tile_expert = jnp.where(jnp.arange(n_tiles) < n_active[0], tile_expert, last).astype(i32)

    seg = (run_lo.reshape(-1).astype(i32), run_g.reshape(-1).astype(i32), run_len.reshape(-1).astype(i32))
    xs = dispatch_rows(u, meta, *seg, p_rows, tm, n_e, r_loc)
    ys = expert_ffn(xs, w1, w3, w2, tile_expert, n_active, tm_e)
    return combine_experts(ys, meta, *seg, h, mod, gate_col, row_group, tm, n_e, r_loc)


def _qkprep_kernel(z_ref, qw_ref, kw_ref, cos_ref, sin_ref, qk_ref, kn_ref, *, n_q, n_kv, hd, scale):
    cos = cos_ref[...]
    sin = sin_ref[...]
    lane = lax.broadcasted_iota(jnp.int32, cos.shape, 1)
    quarter = hd // 4
    first = (lane % (2 * quarter)) < quarter
    for hh in range(n_q + n_kv):
        cols = slice(hh * hd, (hh + 1) * hd)
        x = z_ref[:, cols].astype(F32)
        ms = jnp.mean(x * x, axis=-1, keepdims=True)
        y = x * lax.rsqrt(ms + EPS) * (qw_ref[...] if hh < n_q else kw_ref[...])
        if hh >= n_q:
            kn_ref[:, (hh - n_q) * hd:(hh - n_q + 1) * hd] = y
        partner = jnp.where(first, pltpu.roll(y, hd - quarter, 1), pltpu.roll(y, quarter, 1))
        r = y * cos + partner * sin
        if hh < n_q:
            r = r * scale
        qk_ref[:, cols] = r.astype(BF16)


def qk_prepare(z, q_norm_w, k_norm_w, cos, sin, table_block, tm, *, n_q, n_kv, hd):
    n = z.shape[0]
    w = (n_q + n_kv) * hd
    return pl.pallas_call(
        functools.partial(_qkprep_kernel, n_q=n_q, n_kv=n_kv, hd=hd, scale=hd ** -0.5),
        grid=(n // tm,),
        in_specs=[
            pl.BlockSpec((tm, w), lambda m: (m, 0)),
            pl.BlockSpec((1, hd), lambda m: (0, 0)),
            pl.BlockSpec((1, hd), lambda m: (0, 0)),
            pl.BlockSpec((tm, hd), lambda m: (table_block(m, tm), 0)),
            pl.BlockSpec((tm, hd), lambda m: (table_block(m, tm), 0)),
        ],
        out_specs=[pl.BlockSpec((tm, w), lambda m: (m, 0)), pl.BlockSpec((tm, n_kv * hd), lambda m: (m, 0))],
        out_shape=[jax.ShapeDtypeStruct((n, w), BF16), jax.ShapeDtypeStruct((n, n_kv * hd), F32)],
        compiler_params=_params("arbitrary"),
        name="qk_prep",
    )(z, q_norm_w.reshape(1, hd), k_norm_w.reshape(1, hd), cos, sin)


def _attn_kernel(q_ref, k_ref, v_ref, *rest, group, hd, has_ctx):
    o_ref = rest[-1]
    if has_ctx:
        kc_ref, vc_ref = rest[:2]
        kc = kc_ref[...].astype(BF16)
        vc = vc_ref[...].astype(BF16)
    k = k_ref[...]
    v = v_ref[...]
    for gi in range(group):
        cols = slice(gi * hd, (gi + 1) * hd)
        q = q_ref[:, cols]
        s = _dot_nt(q, k)
        mx = jnp.max(s, axis=-1, keepdims=True)
        if has_ctx:
            sc = _dot_nt(q, kc)
            mx = jnp.maximum(mx, jnp.max(sc, axis=-1, keepdims=True))
        p = jnp.exp(s - mx)
        den = jnp.sum(p, axis=-1, keepdims=True)
        o = _dot(p.astype(BF16), v)
        if has_ctx:
            pc = jnp.exp(sc - mx)
            den = den + jnp.sum(pc, axis=-1, keepdims=True)
            o = o + _dot(pc.astype(BF16), vc)
        o_ref[:, cols] = (o * (1.0 / den)).astype(o_ref.dtype)


def _mix_target(mix, mix_shape, args, in_specs):
    if mix is None:
        return jax.ShapeDtypeStruct(mix_shape, BF16), {}
    args.append(mix)
    in_specs.append(pl.BlockSpec(memory_space=pl.ANY))
    return jax.ShapeDtypeStruct(mix.shape, mix.dtype), {len(args) - 1: 0}


def attention(qk, z, row0, n_batch, seq, *, n_q, n_kv, hd, v_col0, mix, mix_shape, ctx=None):
    group = n_q // n_kv
    tq = _tile(seq, 512)
    nq = seq // tq
    rb_q = row0 // tq
    rb_s = row0 // seq
    in_specs = [
        pl.BlockSpec((tq, group * hd), lambda b, g, i: (rb_q + b * nq + i, g)),
        pl.BlockSpec((seq, hd), lambda b, g, i: (rb_s + b, n_q + g)),
        pl.BlockSpec((seq, hd), lambda b, g, i: (rb_s + b, v_col0 // hd + g)),
    ]
    args = [qk, qk, z]
    if ctx is not None:
        cache_k, cache_v, layer = ctx
        past = cache_k.shape[2]
        c_spec = pl.BlockSpec((None, None, past, hd), lambda b, g, i: (b, layer, 0, g))
        in_specs += [c_spec, c_spec]
        args += [cache_k.reshape(*cache_k.shape[:3], n_kv * hd), cache_v.reshape(*cache_v.shape[:3], n_kv * hd)]
    out_shape, aliases = _mix_target(mix, mix_shape, args, in_specs)
    return pl.pallas_call(
        functools.partial(_attn_kernel, group=group, hd=hd, has_ctx=ctx is not None),
        grid=(n_batch, n_kv, nq),
        in_specs=in_specs,
        out_specs=pl.BlockSpec((tq, group * hd), lambda b, g, i: (rb_q + b * nq + i, g)),
        out_shape=out_shape,
        input_output_aliases=aliases,
        compiler_params=_params("arbitrary", "arbitrary", "arbitrary"),
        name="attn_ctx" if ctx is not None else "attn",
    )(*args)


def _gla_kernel(q_ref, k_ref, v_ref, g_ref, cf_ref, cb_ref, nw_ref, *rest, seq, chunk, has_s0, want_state, scale,
                n_aliased):
    rest = list(rest)
    s0_ref = rest.pop(0) if has_s0 else None
    del rest[:n_aliased]
    o_ref = rest.pop(0)
    sout_ref = rest.pop(0) if want_state else None
    qdf_ref, qdb_ref, oin_ref, kvf_ref, kvb_ref, spf_ref, spb_ref = rest
    n = seq // chunk
    unroll = True if n <= 8 else 8

    row = lax.broadcasted_iota(jnp.int32, (chunk, chunk), 0)
    col = lax.broadcasted_iota(jnp.int32, (chunk, chunk), 1)
    lower = col <= row
    upper = col >= row

    def rows(i):
        return pl.ds(pl.multiple_of(i * chunk, chunk), chunk)

    def intra(i, cum_ref, mask, tot_row, qd_ref, kv_ref):
        sl = rows(i)
        cum = cum_ref[sl, :]
        tot = cum[tot_row:tot_row + 1, :]
        q = q_ref[sl, :].astype(F32) * scale
        k = k_ref[sl, :].astype(F32)
        v = v_ref[sl, :]
        q_dec = (q * jnp.exp(cum)).astype(BF16)
        k_inv = (k * jnp.exp(-cum)).astype(BF16)
        k_end = (k * jnp.exp(tot - cum)).astype(BF16)
        qd_ref[sl, :] = q_dec
        kv_ref[i] = _dot_tn(v, k_end)
        att = jnp.where(mask, _dot_nt(q_dec, k_inv), 0.0).astype(BF16)
        return _dot(att, v)

    def phase1(i, carry):
        oin_ref[rows(i), :] = (intra(i, cf_ref, lower, chunk - 1, qdf_ref, kvf_ref)
                               + intra(i, cb_ref, upper, 0, qdb_ref, kvb_ref))
        return carry

    lax.fori_loop(0, n, phase1, 0, unroll=unroll)

    def scan(reverse, cum_ref, tot_row, kv_ref, sp_ref, init):
        def step(j, st):
            i = n - 1 - j if reverse else j
            tot = cum_ref[pl.ds(i * chunk + tot_row, 1), :]
            sp_ref[i] = st.astype(BF16)
            return jnp.exp(tot) * st + kv_ref[i]

        return lax.fori_loop(0, n, step, init)

    zero = jnp.zeros(kvf_ref.shape[1:], F32)
    st_f = scan(False, cf_ref, chunk - 1, kvf_ref, spf_ref, s0_ref[0].T if has_s0 else zero)
    st_b = scan(True, cb_ref, 0, kvb_ref, spb_ref, s0_ref[1].T if has_s0 else zero)
    if want_state:
        sout_ref[0] = st_f.T
        sout_ref[1] = st_b.T

    def phase3(i, carry):
        sl = rows(i)
        o = oin_ref[sl, :] + _dot_nt(qdf_ref[sl, :], spf_ref[i]) + _dot_nt(qdb_ref[sl, :], spb_ref[i])
        ms = jnp.mean(o * o, axis=-1, keepdims=True)
        y = o * lax.rsqrt(ms + EPS) * nw_ref[...]
        o_ref[sl, :] = (y * _silu(g_ref[sl, :].astype(F32))).astype(o_ref.dtype)
        return carry

    lax.fori_loop(0, n, phase3, 0, unroll=unroll)


def gla(z, la, norm_w, row0, n_batch, seq, mix, mix_col0, *, heads, dk, dv, q_col0, k_col0, v_col0, g_col0, s0=None,
        new_state=None):
    rb = row0 // seq
    n_chunks = seq // GLA_CHUNK
    want_state = new_state is not None
    in_specs = [
        pl.BlockSpec((seq, dk), lambda b, hh: (rb + b, q_col0 // dk + hh)),
        pl.BlockSpec((seq, dk), lambda b, hh: (rb + b, k_col0 // dk + hh)),
        pl.BlockSpec((seq, dv), lambda b, hh: (rb + b, v_col0 // dv + hh)),
        pl.BlockSpec((seq, dv), lambda b, hh: (rb + b, g_col0 // dv + hh)),
        pl.BlockSpec((seq, dk), lambda b, hh: (rb + b, hh)),
        pl.BlockSpec((seq, dk), lambda b, hh: (rb + b, heads + hh)),
        pl.BlockSpec((1, dv), lambda b, hh: (0, 0)),
    ]
    args = [z, z, z, z, la, la, norm_w.reshape(1, dv)]
    if s0 is not None:
        state, layer = s0
        in_specs += [pl.BlockSpec((None, None, 2, None, dk, dv), lambda b, hh: (b, layer, 0, hh, 0, 0))]
        args += [state]
    n_fixed = len(args)
    mix_shape, aliases = _mix_target(mix, None, args, in_specs)
    out_specs = [pl.BlockSpec((seq, dv), lambda b, hh: (rb + b, mix_col0 // dv + hh))]
    out_shape = [mix_shape]
    if want_state:
        states, layer_out, depth = new_state
        out_specs += [pl.BlockSpec((None, None, 2, None, dk, dv), lambda b, hh: (b, layer_out, 0, hh, 0, 0))]
        out_shape += [jax.ShapeDtypeStruct((n_batch, depth, 2, heads, dk, dv), F32)]
        if states is not None:
            args.append(states)
            in_specs.append(pl.BlockSpec(memory_space=pl.ANY))
            aliases[len(args) - 1] = 1
    res = pl.pallas_call(
        functools.partial(_gla_kernel, seq=seq, chunk=GLA_CHUNK, has_s0=s0 is not None, want_state=want_state,
                          scale=dk ** -0.5, n_aliased=len(args) - n_fixed),
        grid=(n_batch, heads),
        in_specs=in_specs,
        out_specs=out_specs,
        out_shape=out_shape,
        input_output_aliases=aliases,
        scratch_shapes=[pltpu.VMEM((seq, dk), BF16), pltpu.VMEM((seq, dk), BF16), pltpu.VMEM((seq, dv), F32),
                        pltpu.VMEM((n_chunks, dv, dk), F32), pltpu.VMEM((n_chunks, dv, dk), F32),
                        pltpu.VMEM((n_chunks, dv, dk), BF16), pltpu.VMEM((n_chunks, dv, dk), BF16)],
        compiler_params=_params("arbitrary", "arbitrary"),
        name="gla_state" if want_state else "gla",
    )(*args)
    return res if want_state else (res[0], None)


def _rope_tables(seq, hd, lead):
    axis_dim = hd // 2
    t = jnp.arange(seq)
    rowp = (t // GRID_W).astype(F32)
    colp = (t % GRID_W).astype(F32)
    inv = ROPE_THETA ** (-jnp.arange(axis_dim // 2, dtype=F32) * 2.0 / axis_dim)
    ang_r = rowp[:, None] * inv
    ang_c = colp[:, None] * inv
    cos = jnp.concatenate([jnp.cos(ang_r), jnp.cos(ang_r), jnp.cos(ang_c), jnp.cos(ang_c)], axis=-1)
    sin = jnp.concatenate([-jnp.sin(ang_r), jnp.sin(ang_r), -jnp.sin(ang_c), jnp.sin(ang_c)], axis=-1)
    cos = jnp.concatenate([jnp.ones((lead, hd), F32), cos], axis=0)
    sin = jnp.concatenate([jnp.zeros((lead, hd), F32), sin], axis=0)
    return cos, sin


def kernel(x_prompt, x_sample, cache_k, cache_v, state_gla, c, c_ctx, norm1_w, norm2_w, w_ada, b_ada, w_in,
           q_norm_w, k_norm_w, gla_up, gla_bias, gla_norm_w, w_out, ffn_w1, ffn_w3, ffn_w2, router_w, moe_w1,
           moe_w3, moe_w2):
    batch, seq1, d = x_prompt.shape
    dec_batch, seq2, _ = x_sample.shape
    depth = w_in.shape[0]
    hd = q_norm_w.shape[-1]
    n_kv = cache_k.shape[3]
    n_q = d // 2 // hd
    heads, dk, dv = state_gla.shape[3:]
    rank = gla_up.shape[2]
    n_experts = router_w.shape[-1]
    attn_w, kv_w, key_w, gla_w = n_q * hd, n_kv * hd, heads * dk, heads * dv
    main_cols = attn_w + 2 * kv_w + 2 * key_w + 2 * gla_w
    q_col0 = attn_w + 2 * kv_w
    k_col0 = q_col0 + key_w
    v_col0 = k_col0 + key_w
    g_col0 = v_col0 + gla_w
    n1, n2 = batch * seq1, dec_batch * seq2
    n = n1 + n2
    assert hd == LANES and n1 % seq2 == 0 and 2 * rank <= LANES and n_experts <= LANES

    def row_group(m, tm):
        tok = m * tm
        return jnp.where(tok < n1, 0, 1 + (tok - n1) // seq2)

    def table_block(m, tm):
        tok = m * tm
        return jnp.where(tok < n1, 0, 1 + ((tok - n1) % seq2) // tm)

    n_rows = -(-(1 + dec_batch) // SUBLANES) * SUBLANES
    cond = jnp.zeros((n_rows, d), F32).at[0].set(c_ctx).at[1:1 + dec_batch].set(c)
    mod_all = ada_modulation(cond, w_ada, b_ada)

    tm_row = _tile(math.gcd(n1, seq2), 256)
    tm_prep = _tile(math.gcd(n1, seq2), 512)
    m_cap = math.gcd(n1, seq2)
    cos, sin = _rope_tables(seq2, hd, tm_prep)

    w_in_t = jnp.swapaxes(w_in, 1, 2)
    h = jnp.concatenate([x_prompt.reshape(n1, d), x_sample.reshape(n2, d)], axis=0)
    new_k, new_v, new_s = [], [], None
    for l in range(depth):
        mod = mod_all[l].reshape(n_rows, 1, N_MOD * d)
        w_lr = jnp.zeros((d, LANES), BF16).at[:, :2 * rank].set(w_in_t[l, main_cols:, :].T.astype(BF16))
        up = jnp.zeros((LANES, 2 * key_w), F32)
        up = up.at[:rank, :key_w].set(gla_up[l, 0]).at[rank:2 * rank, key_w:].set(gla_up[l, 1])
        bias = gla_bias[l].reshape(1, 2 * key_w)

        u, la = norm_modulate(h, norm1_w[l], mod, 1, 0, row_group, tm_row, mode="decay", extra=(w_lr, up, bias))
        z = matmul(u, [w_in_t], l, mode="plain", out_dtype=BF16, m_cap=m_cap, n_cols=main_cols, w_transposed=True)

        qk, kn = qk_prepare(z, q_norm_w[l], k_norm_w[l], cos, sin, table_block, tm_prep, n_q=n_q, n_kv=n_kv, hd=hd)
        new_k.append(kn[:n1].reshape(batch, seq1, n_kv, hd))
        new_v.append(z[:n1, attn_w + kv_w:attn_w + 2 * kv_w].astype(F32).reshape(batch, seq1, n_kv, hd))

        attn_kw = dict(n_q=n_q, n_kv=n_kv, hd=hd, v_col0=attn_w + kv_w, mix_shape=(n, attn_w + gla_w))
        mix = attention(qk, z, 0, batch, seq1, mix=None, **attn_kw)
        mix = attention(qk, z, n1, dec_batch, seq2, mix=mix, ctx=(cache_k, cache_v, l), **attn_kw)
        gla_kw = dict(heads=heads, dk=dk, dv=dv, q_col0=q_col0, k_col0=k_col0, v_col0=v_col0, g_col0=g_col0)
        mix, new_s = gla(z, la, gla_norm_w[l], 0, batch, seq1, mix, attn_w, new_state=(new_s, l, depth), **gla_kw)
        mix, _ = gla(z, la, gla_norm_w[l], n1, dec_batch, seq2, mix, attn_w, s0=(state_gla, l), **gla_kw)

        h = matmul(mix, [w_out], l, mode="resid", out_dtype=F32, m_cap=m_cap, h=h, mod=mod, gate_col=2 * d,
                   row_group=row_group)

        i = l // 2
        if l % 2 == 0:
            (u2,) = norm_modulate(h, norm2_w[l], mod, 4, 3, row_group, tm_row, mode="plain", extra=())
            hid = matmul(u2, [ffn_w1, ffn_w3], i, mode="swiglu", out_dtype=BF16, m_cap=m_cap)
            h = matmul(hid, [ffn_w2], i, mode="resid", out_dtype=F32, m_cap=m_cap, h=h, mod=mod, gate_col=5 * d,
                       row_group=row_group)
        else:
            rw = jnp.zeros((d, LANES), F32).at[:, :n_experts].set(router_w[i])
            xp, meta, counts = norm_modulate(h, norm2_w[l], mod, 4, 3, row_group, tm_row, mode="router",
                                             extra=(rw,), n_experts=n_experts)
            h = moe_ffn(xp, meta, counts, h, mod, 5 * d, row_group, cast_bf16(moe_w1, i), cast_bf16(moe_w3, i),
                        cast_bf16(moe_w2, i), tm_row, tm_prep)

    y_prompt = h[:n1].reshape(batch, seq1, d)
    y_sample = h[n1:].reshape(dec_batch, seq2, d)
    return (y_prompt, y_sample, jnp.stack(new_k, axis=1), jnp.stack(new_v, axis=1),
            new_s.astype(x_prompt.dtype))
```

```python
import functools
import math

import jax
import jax.numpy as jnp
from jax import lax
from jax.experimental import pallas as pl
from jax.experimental.pallas import tpu as pltpu

F32 = jnp.float32
BF16 = jnp.bfloat16

EPS = 1e-6
GRID_W = 64
GLA_CHUNK = 64
GLA_TAU = 16.0
ROPE_THETA = 10000.0
N_MOD = 6
LANES = 128
SUBLANES = 8
VMEM_LIMIT_BYTES = 56 * 1024 * 1024


def _params(*semantics):
    return pltpu.CompilerParams(dimension_semantics=semantics, vmem_limit_bytes=VMEM_LIMIT_BYTES)


def _tile(n, cap):
    t = cap
    while t > 1 and n % t:
        t //= 2
    return t


def _dot(a, b):
    return jnp.dot(a, b, preferred_element_type=F32)


def _dot_nt(a, b):
    return lax.dot_general(a, b, (((1,), (1,)), ((), ())), preferred_element_type=F32)


def _dot_tn(a, b):
    return lax.dot_general(a, b, (((0,), (0,)), ((), ())), preferred_element_type=F32)


def _split_bf16(x):
    hi = x.astype(BF16)
    lo = (x - hi.astype(F32)).astype(BF16)
    return hi, lo


def _silu(x):
    return x * jax.nn.sigmoid(x)


def _ada_kernel(c_ref, w_ref, b_ref, o_ref):
    s = _silu(c_ref[...]).astype(BF16)
    o_ref[...] = _dot(s, w_ref[...].astype(BF16)) + b_ref[...]


def ada_modulation(cond, w_ada, b_ada):
    depth, d, n = w_ada.shape
    r = cond.shape[0]
    tn = _tile(n, 1024)
    return pl.pallas_call(
        _ada_kernel,
        grid=(depth, n // tn),
        in_specs=[
            pl.BlockSpec((r, d), lambda l, j: (0, 0)),
            pl.BlockSpec((None, d, tn), lambda l, j: (l, 0, j)),
            pl.BlockSpec((None, 1, tn), lambda l, j: (l, 0, j)),
        ],
        out_specs=pl.BlockSpec((None, r, tn), lambda l, j: (l, 0, j)),
        out_shape=jax.ShapeDtypeStruct((depth, r, n), F32),
        compiler_params=_params("arbitrary", "arbitrary"),
        name="ada",
    )(cond, w_ada, b_ada.reshape(depth, 1, n))


def _norm_kernel(h_ref, nw_ref, sc_ref, sh_ref, *rest, mode, n_experts):
    x = h_ref[...]
    ms = jnp.mean(x * x, axis=-1, keepdims=True)
    y = x * lax.rsqrt(ms + EPS) * nw_ref[...]
    u = y * (1.0 + sc_ref[...]) + sh_ref[...]
    if mode == "decay":
        wlr_ref, up_ref, bias_ref, u_ref, la_ref = rest
        ub = u.astype(BF16)
        u_ref[...] = ub
        lr = _dot(ub, wlr_ref[...].astype(BF16))
        lr_hi, lr_lo = _split_bf16(lr)
        up_hi, up_lo = _split_bf16(up_ref[...])
        logits = _dot(lr_hi, up_hi) + (_dot(lr_lo, up_hi) + _dot(lr_hi, up_lo)) + bias_ref[...]
        log_sig = jnp.minimum(logits, 0.0) - jnp.log1p(jnp.exp(-jnp.abs(logits)))
        la = log_sig * (1.0 / GLA_TAU)
        tm = la.shape[0]
        half = la.shape[1] // 2
        t_row = lax.broadcasted_iota(jnp.int32, (tm, tm), 0)
        t_col = lax.broadcasted_iota(jnp.int32, (tm, tm), 1)
        same_chunk = (t_row // GLA_CHUNK) == (t_col // GLA_CHUNK)
        tri_f = jnp.logical_and(same_chunk, t_col <= t_row).astype(BF16)
        tri_b = jnp.logical_and(same_chunk, t_col >= t_row).astype(BF16)
        la_hi, la_lo = _split_bf16(la)
        la_ref[:, :half] = _dot(tri_f, la_hi[:, :half]) + _dot(tri_f, la_lo[:, :half])
        la_ref[:, half:] = _dot(tri_b, la_hi[:, half:]) + _dot(tri_b, la_lo[:, half:])
    elif mode == "plain":
        (u_ref,) = rest
        u_ref[...] = u.astype(BF16)
    else:
        rw_ref, u_ref, meta_ref, cnt_ref = rest
        u_ref[...] = u.astype(BF16)
        u_hi, u_lo = _split_bf16(u)
        rw_hi, rw_lo = _split_bf16(rw_ref[...])
        logits = _dot(u_hi, rw_hi) + (_dot(u_lo, rw_hi) + _dot(u_hi, rw_lo))
        lane = lax.broadcasted_iota(jnp.int32, logits.shape, 1)
        neg = jnp.float32(-jnp.inf)
        logits = jnp.where(lane < n_experts, logits, neg)
        v1 = jnp.max(logits, axis=-1, keepdims=True)
        i1 = jnp.min(jnp.where(logits == v1, lane, LANES), axis=-1, keepdims=True)
        rest_l = jnp.where(lane == i1, neg, logits)
        v2 = jnp.max(rest_l, axis=-1, keepdims=True)
        i2 = jnp.min(jnp.where(rest_l == v2, lane, LANES), axis=-1, keepdims=True)
        e2 = jnp.exp(v2 - v1)
        g1 = 1.0 / (1.0 + e2)
        g2 = e2 / (1.0 + e2)

        tm = logits.shape[0]
        chosen = jnp.logical_or(lane == i1, lane == i2)
        t_row = lax.broadcasted_iota(jnp.int32, (tm, tm), 0)
        t_col = lax.broadcasted_iota(jnp.int32, (tm, tm), 1)
        before = _dot((t_col < t_row).astype(BF16), chosen.astype(BF16))
        r1 = jnp.sum(jnp.where(lane == i1, before, 0.0), axis=-1, keepdims=True)
        r2 = jnp.sum(jnp.where(lane == i2, before, 0.0), axis=-1, keepdims=True)
        cnt_ref[...] = jnp.broadcast_to(jnp.sum(chosen.astype(F32), axis=0, keepdims=True), cnt_ref.shape)
        fields = (i1.astype(F32), i2.astype(F32), g1, g2, r1, r2)
        meta = jnp.zeros(logits.shape, F32)
        for pos, val in enumerate(fields):
            meta = jnp.where(lane == pos, val, meta)
        meta_ref[...] = meta


def norm_modulate(h, norm_w, mod, sc_col, sh_col, row_group, tm, *, mode, extra, n_experts=0):
    n, d = h.shape
    in_specs = [
        pl.BlockSpec((tm, d), lambda m: (m, 0)),
        pl.BlockSpec((1, d), lambda m: (0, 0)),
        pl.BlockSpec((None, 1, d), lambda m: (row_group(m, tm), 0, sc_col)),
        pl.BlockSpec((None, 1, d), lambda m: (row_group(m, tm), 0, sh_col)),
    ]
    if mode == "decay":
        wlr, up, bias = extra
        in_specs += [
            pl.BlockSpec(wlr.shape, lambda m: (0, 0)),
            pl.BlockSpec(up.shape, lambda m: (0, 0)),
            pl.BlockSpec(bias.shape, lambda m: (0, 0)),
        ]
        side = up.shape[1]
    elif mode == "router":
        (rw,) = extra
        in_specs += [pl.BlockSpec(rw.shape, lambda m: (0, 0))]
    if mode == "router":
        out_specs = [pl.BlockSpec((tm, d), lambda m: (m, 0)),
                     pl.BlockSpec((tm, LANES), lambda m: (m, 0)),
                     pl.BlockSpec((SUBLANES, LANES), lambda m: (m, 0))]
        out_shape = [jax.ShapeDtypeStruct((n, d), BF16), jax.ShapeDtypeStruct((n, LANES), F32),
                     jax.ShapeDtypeStruct((n // tm * SUBLANES, LANES), F32)]
    else:
        out_specs = [pl.BlockSpec((tm, d), lambda m: (m, 0))]
        out_shape = [jax.ShapeDtypeStruct((n, d), BF16)]
        if mode == "decay":
            out_specs += [pl.BlockSpec((tm, side), lambda m: (m, 0))]
            out_shape += [jax.ShapeDtypeStruct((n, side), F32)]
    return pl.pallas_call(
        functools.partial(_norm_kernel, mode=mode, n_experts=n_experts),
        grid=(n // tm,),
        in_specs=in_specs,
        out_specs=out_specs,
        out_shape=out_shape,
        compiler_params=_params("arbitrary"),
        name="norm_" + mode,
    )(h, norm_w.reshape(1, d), mod, mod, *extra)


def _mm_kernel(*refs, mode, n_w, w_transposed):
    mm = _dot_nt if w_transposed else _dot
    x_ref = refs[0]
    w_refs = refs[1:1 + n_w]
    wb_refs = refs[len(refs) - n_w:]
    rest = refs[1 + n_w:len(refs) - n_w]

    @pl.when(pl.program_id(1) == 0)
    def _():
        for w_ref, wb_ref in zip(w_refs, wb_refs):
            wb_ref[...] = w_ref[...].astype(BF16)

    x = x_ref[...]
    if mode == "swiglu":
        (o_ref,) = rest
        o_ref[...] = (_silu(mm(x, wb_refs[0][...])) * mm(x, wb_refs[1][...])).astype(o_ref.dtype)
    elif mode == "resid":
        h_ref, g_ref, o_ref = rest
        o_ref[...] = h_ref[...] + g_ref[...] * mm(x, wb_refs[0][...])
    else:
        (o_ref,) = rest
        o_ref[...] = mm(x, wb_refs[0][...]).astype(o_ref.dtype)


def _mm_tiles(m_cap, k, n_cols, n_w, mode):
    tn = _tile(n_cols, 512 if (n_w > 1 or k > 4096) else 1024)
    tm = _tile(m_cap, 512 if (mode == "resid" or k > 4096) else 1024)
    io_bytes = {"plain": 2, "swiglu": 2, "resid": 8}[mode]
    est = n_w * k * tn * (4 + 2) + 2 * tm * k * 2 + 2 * tm * tn * io_bytes + tm * tn * 4 * n_w
    assert est <= VMEM_LIMIT_BYTES, (est, tm, tn)
    return tm, tn


def matmul(x, ws, layer, *, mode, out_dtype, m_cap, n_cols=None, w_transposed=False, h=None, mod=None,
           gate_col=None, row_group=None):
    m_tot, k = x.shape
    n_cols = ws[0].shape[1 if w_transposed else 2] if n_cols is None else n_cols
    tm, tn = _mm_tiles(m_cap, k, n_cols, len(ws), mode)
    if w_transposed:
        w_spec = pl.BlockSpec((None, tn, k), lambda n, m: (layer, n, 0), pipeline_mode=pl.Buffered(1))
    else:
        w_spec = pl.BlockSpec((None, k, tn), lambda n, m: (layer, 0, n), pipeline_mode=pl.Buffered(1))
    in_specs = [pl.BlockSpec((tm, k), lambda n, m: (m, 0))] + [w_spec] * len(ws)
    args = [x, *ws]
    if mode == "resid":
        gate_blk = gate_col // tn
        in_specs += [
            pl.BlockSpec((tm, tn), lambda n, m: (m, n)),
            pl.BlockSpec((None, 1, tn), lambda n, m: (row_group(m, tm), 0, gate_blk + n)),
        ]
        args += [h, mod]
    return pl.pallas_call(
        functools.partial(_mm_kernel, mode=mode, n_w=len(ws), w_transposed=w_transposed),
        grid=(n_cols // tn, m_tot // tm),
        in_specs=in_specs,
        out_specs=pl.BlockSpec((tm, tn), lambda n, m: (m, n)),
        out_shape=jax.ShapeDtypeStruct((m_tot, n_cols), out_dtype),
        scratch_shapes=[pltpu.VMEM((tn, k) if w_transposed else (k, tn), BF16)] * len(ws),
        compiler_params=_params("arbitrary", "arbitrary"),
        name="mm_" + mode,
    )(*args)


SEG_ALIGN = 16


def _run_copies(t, n_experts, lo_ref, g_ref, len_ref, make_copy, wait):
    for e in range(n_experts):
        base = t * n_experts + e

        def body(c, carry, base=base):
            lo = pl.multiple_of(lo_ref[base] + c * SEG_ALIGN, SEG_ALIGN)
            g = pl.multiple_of(g_ref[base] + c * SEG_ALIGN, SEG_ALIGN)
            cp = make_copy(lo, g)
            if wait:
                cp.wait()
            else:
                cp.start()
            return carry

        lax.fori_loop(0, len_ref[base] // SEG_ALIGN, body, 0)


def _dispatch_kernel(lo_ref, g_ref, len_ref, u_ref, meta_ref, *rest, n_experts):
    xs_hbm, sorted_ref, sem = rest[-3:]
    meta = meta_ref[...]
    lane = lax.broadcasted_iota(jnp.int32, meta.shape, 1)
    d1 = jnp.sum(jnp.where(lane == 6, meta, 0.0), axis=-1, keepdims=True).astype(jnp.int32)
    d2 = jnp.sum(jnp.where(lane == 7, meta, 0.0), axis=-1, keepdims=True).astype(jnp.int32)
    slot = lax.broadcasted_iota(jnp.int32, (meta.shape[0], sorted_ref.shape[0]), 1)
    onehot = jnp.logical_or(slot == d1, slot == d2).astype(BF16)
    sorted_ref[...] = _dot_tn(onehot, u_ref[...]).astype(BF16)

    def make_copy(lo, g):
        return pltpu.make_async_copy(sorted_ref.at[pl.ds(lo, SEG_ALIGN)], xs_hbm.at[pl.ds(g, SEG_ALIGN)], sem)

    for wait in (False, True):
        _run_copies(pl.program_id(0), n_experts, lo_ref, g_ref, len_ref, make_copy, wait)


def dispatch_rows(u, meta, seg_lo, seg_g, seg_len, p_rows, tm, n_experts, r_loc):
    n, d = u.shape
    return pl.pallas_call(
        functools.partial(_dispatch_kernel, n_experts=n_experts),
        grid_spec=pltpu.PrefetchScalarGridSpec(
            num_scalar_prefetch=3,
            grid=(n // tm,),
            in_specs=[
                pl.BlockSpec((tm, d), lambda t, lo, g, ln: (t, 0)),
                pl.BlockSpec((tm, LANES), lambda t, lo, g, ln: (t, 0)),
                pl.BlockSpec(memory_space=pl.ANY),
            ],
            out_specs=pl.BlockSpec(memory_space=pl.ANY),
            scratch_shapes=[pltpu.VMEM((r_loc, d), BF16), pltpu.SemaphoreType.DMA(())],
        ),
        out_shape=jax.ShapeDtypeStruct((p_rows, d), BF16),
        input_output_aliases={5: 0},
        compiler_params=_params("arbitrary"),
        name="moe_dispatch",
    )(seg_lo, seg_g, seg_len, u, meta, jnp.zeros((p_rows, d), BF16))


def _expert_mm_kernel(te_ref, na_ref, new_ref, x_ref, *rest, n_w):
    w_refs, o_ref, wb_refs = rest[:n_w], rest[n_w], rest[n_w + 1:]
    i = pl.program_id(1)

    @pl.when(new_ref[i] == 1)
    def _():
        for w_ref, wb_ref in zip(w_refs, wb_refs):
            wb_ref[...] = w_ref[...].astype(BF16)

    @pl.when(i < na_ref[0])
    def _():
        x = x_ref[...]
        if n_w == 2:
            o_ref[...] = (_silu(_dot(x, wb_refs[0][...])) * _dot(x, wb_refs[1][...])).astype(o_ref.dtype)
        else:
            o_ref[...] = _dot(x, wb_refs[0][...]).astype(o_ref.dtype)

    @pl.when(i >= na_ref[0])
    def _():
        o_ref[...] = jnp.zeros_like(o_ref)


def _expert_matmul(x, ws, layer, tn, tile_expert, n_active, new_expert, tm, name):
    p_rows, k = x.shape
    n_cols = ws[0].shape[3]
    w_spec = pl.BlockSpec((None, None, k, tn), lambda j, i, te, na, new: (layer, te[i], 0, j))
    return pl.pallas_call(
        functools.partial(_expert_mm_kernel, n_w=len(ws)),
        grid_spec=pltpu.PrefetchScalarGridSpec(
            num_scalar_prefetch=3,
            grid=(n_cols // tn, p_rows // tm),
            in_specs=[pl.BlockSpec((tm, k), lambda j, i, te, na, new: (i, 0))] + [w_spec] * len(ws),
            out_specs=pl.BlockSpec((tm, tn), lambda j, i, te, na, new: (i, j)),
            scratch_shapes=[pltpu.VMEM((k, tn), BF16)] * len(ws),
        ),
        out_shape=jax.ShapeDtypeStruct((p_rows, n_cols), BF16),
        compiler_params=_params("arbitrary", "arbitrary"),
        name=name,
    )(tile_expert, n_active, new_expert, x, *ws)


def expert_ffn(xs, w1, w3, w2, layer, tile_expert, n_active, new_expert, tm):
    f, d = w2.shape[2:]
    route = (tile_expert, n_active, new_expert, tm)
    hid = _expert_matmul(xs, [w1, w3], layer, _tile(f, 512), *route, "expert_swiglu")
    return _expert_matmul(hid, [w2], layer, _tile(d, 1024), *route, "expert_down")


def _combine_kernel(lo_ref, g_ref, len_ref, ys_hbm, meta_ref, h_ref, gate_ref, o_ref, runs_ref, sem, *, n_experts):
    t = pl.program_id(0)
    buf = t % 2

    def copies(tile, b, wait):
        def make_copy(lo, g):
            return pltpu.make_async_copy(ys_hbm.at[pl.ds(g, SEG_ALIGN)], runs_ref.at[b, pl.ds(lo, SEG_ALIGN)],
                                         sem.at[b])

        _run_copies(tile, n_experts, lo_ref, g_ref, len_ref, make_copy, wait)

    @pl.when(t == 0)
    def _():
        runs_ref[...] = jnp.zeros_like(runs_ref)
        copies(t, buf, False)

    @pl.when(t + 1 < pl.num_programs(0))
    def _():
        copies(t + 1, 1 - buf, False)

    copies(t, buf, True)

    meta = meta_ref[...]
    lane = lax.broadcasted_iota(jnp.int32, meta.shape, 1)

    def field(k):
        return jnp.sum(jnp.where(lane == k, meta, 0.0), axis=-1, keepdims=True)

    slot = lax.broadcasted_iota(jnp.int32, (meta.shape[0], runs_ref.shape[1]), 1)
    runs = runs_ref[buf]
    y1 = _dot((slot == field(6).astype(jnp.int32)).astype(BF16), runs)
    y2 = _dot((slot == field(7).astype(jnp.int32)).astype(BF16), runs)
    o_ref[...] = h_ref[...] + gate_ref[...] * (field(2) * y1 + field(3) * y2)


def combine_experts(ys, meta, seg_lo, seg_g, seg_len, h, mod, gate_col, row_group, tm, n_experts, r_loc):
    n, d = h.shape
    return pl.pallas_call(
        functools.partial(_combine_kernel, n_experts=n_experts),
        grid_spec=pltpu.PrefetchScalarGridSpec(
            num_scalar_prefetch=3,
            grid=(n // tm,),
            in_specs=[
                pl.BlockSpec(memory_space=pl.ANY),
                pl.BlockSpec((tm, LANES), lambda t, lo, g, ln: (t, 0)),
                pl.BlockSpec((tm, d), lambda t, lo, g, ln: (t, 0)),
                pl.BlockSpec((None, 1, d), lambda t, lo, g, ln: (row_group(t, tm), 0, gate_col // d)),
            ],
            out_specs=pl.BlockSpec((tm, d), lambda t, lo, g, ln: (t, 0)),
            scratch_shapes=[pltpu.VMEM((2, r_loc, d), BF16), pltpu.SemaphoreType.DMA((2,))],
        ),
        out_shape=jax.ShapeDtypeStruct((n, d), F32),
        compiler_params=_params("arbitrary"),
        name="moe_combine",
    )(seg_lo, seg_g, seg_len, ys, meta, h, mod)


def moe_ffn(u, meta, counts, h, mod, gate_col, row_group, w1, w3, w2, layer, tm, tm_e):
    n, d = h.shape
    n_e = w1.shape[1]
    n_tt = n // tm
    i32 = jnp.int32
    experts = meta[:, 0:2].astype(i32)
    ranks = meta[:, 4:6].astype(i32)
    cnt = counts.reshape(n_tt, SUBLANES, LANES)[:, 0, :n_e].astype(i32)
    run_len = (cnt + SEG_ALIGN - 1) // SEG_ALIGN * SEG_ALIGN
    run_lo = jnp.cumsum(run_len, axis=1) - run_len
    region = (jnp.sum(run_len, axis=0) + tm_e - 1) // tm_e * tm_e
    region_end = jnp.cumsum(region)
    run_g = (region_end - region)[None, :] + jnp.cumsum(run_len, axis=0) - run_len
    onehot = experts[:, :, None] == jnp.arange(n_e)[None, None, :]
    lo_tok = jnp.repeat(run_lo, tm, axis=0)[:, None, :]
    dest = jnp.sum(jnp.where(onehot, lo_tok, 0), axis=-1) + ranks
    meta = lax.dynamic_update_slice(meta, dest.astype(F32), (0, 6))

    r_loc = 2 * tm + n_e * SEG_ALIGN
    p_rows = -(-(2 * n + n_tt * n_e * SEG_ALIGN + n_e * tm_e) // tm_e) * tm_e
    n_tiles = p_rows // tm_e
    n_active = (region_end[-1] // tm_e).reshape(1).astype(i32)
    tile_row = jnp.arange(n_tiles) * tm_e
    tile_expert = jnp.sum(tile_row[:, None] >= region_end[None, :], axis=1)
    last = jnp.sum(tile_expert * (jnp.arange(n_tiles) == n_active[0] - 1))
    tile_expert = jnp.where(jnp.arange(n_tiles) < n_active[0], tile_expert, last).astype(i32)
    new_expert = jnp.concatenate([jnp.ones((1,), i32), (tile_expert[1:] != tile_expert[:-1]).astype(i32)])

    seg = (run_lo.reshape(-1).astype(i32), run_g.reshape(-1).astype(i32), run_len.reshape(-1).astype(i32))
    xs = dispatch_rows(u, meta, *seg, p_rows, tm, n_e, r_loc)
    ys = expert_ffn(xs, w1, w3, w2, layer, tile_expert, n_active, new_expert, tm_e)
    return combine_experts(ys, meta, *seg, h, mod, gate_col, row_group, tm, n_e, r_loc)


def _qkprep_kernel(z_ref, qw_ref, kw_ref, cos_ref, sin_ref, qk_ref, kn_ref, *, n_q, n_kv, hd, scale):
    cos = cos_ref[...]
    sin = sin_ref[...]
    lane = lax.broadcasted_iota(jnp.int32, cos.shape, 1)
    quarter = hd // 4
    first = (lane % (2 * quarter)) < quarter
    for hh in range(n_q + n_kv):
        cols = slice(hh * hd, (hh + 1) * hd)
        x = z_ref[:, cols].astype(F32)
        ms = jnp.mean(x * x, axis=-1, keepdims=True)
        y = x * lax.rsqrt(ms + EPS) * (qw_ref[...] if hh < n_q else kw_ref[...])
        if hh >= n_q:
            kn_ref[:, (hh - n_q) * hd:(hh - n_q + 1) * hd] = y
        partner = jnp.where(first, pltpu.roll(y, hd - quarter, 1), pltpu.roll(y, quarter, 1))
        r = y * cos + partner * sin
        if hh < n_q:
            r = r * scale
        qk_ref[:, cols] = r.astype(BF16)


def qk_prepare(z, q_norm_w, k_norm_w, cos, sin, table_block, tm, *, n_q, n_kv, hd):
    n = z.shape[0]
    w = (n_q + n_kv) * hd
    return pl.pallas_call(
        functools.partial(_qkprep_kernel, n_q=n_q, n_kv=n_kv, hd=hd, scale=hd ** -0.5),
        grid=(n // tm,),
        in_specs=[
            pl.BlockSpec((tm, w), lambda m: (m, 0)),
            pl.BlockSpec((1, hd), lambda m: (0, 0)),
            pl.BlockSpec((1, hd), lambda m: (0, 0)),
            pl.BlockSpec((tm, hd), lambda m: (table_block(m, tm), 0)),
            pl.BlockSpec((tm, hd), lambda m: (table_block(m, tm), 0)),
        ],
        out_specs=[pl.BlockSpec((tm, w), lambda m: (m, 0)), pl.BlockSpec((tm, n_kv * hd), lambda m: (m, 0))],
        out_shape=[jax.ShapeDtypeStruct((n, w), BF16), jax.ShapeDtypeStruct((n, n_kv * hd), F32)],
        compiler_params=_params("arbitrary"),
        name="qk_prep",
    )(z, q_norm_w.reshape(1, hd), k_norm_w.reshape(1, hd), cos, sin)


def _attn_kernel(q_ref, k_ref, v_ref, *rest, group, hd, has_ctx):
    o_ref = rest[-1]
    if has_ctx:
        kc_ref, vc_ref = rest[:2]
        kc = kc_ref[...].astype(BF16)
        vc = vc_ref[...].astype(BF16)
    k = k_ref[...]
    v = v_ref[...]
    for gi in range(group):
        cols = slice(gi * hd, (gi + 1) * hd)
        q = q_ref[:, cols]
        s = _dot_nt(q, k)
        mx = jnp.max(s, axis=-1, keepdims=True)
        if has_ctx:
            sc = _dot_nt(q, kc)
            mx = jnp.maximum(mx, jnp.max(sc, axis=-1, keepdims=True))
        p = jnp.exp(s - mx)
        den = jnp.sum(p, axis=-1, keepdims=True)
        o = _dot(p.astype(BF16), v)
        if has_ctx:
            pc = jnp.exp(sc - mx)
            den = den + jnp.sum(pc, axis=-1, keepdims=True)
            o = o + _dot(pc.astype(BF16), vc)
        o_ref[:, cols] = (o * (1.0 / den)).astype(o_ref.dtype)


def _mix_target(mix, mix_shape, args, in_specs):
    if mix is None:
        return jax.ShapeDtypeStruct(mix_shape, BF16), {}
    args.append(mix)
    in_specs.append(pl.BlockSpec(memory_space=pl.ANY))
    return jax.ShapeDtypeStruct(mix.shape, mix.dtype), {len(args) - 1: 0}


def attention(qk, z, row0, n_batch, seq, *, n_q, n_kv, hd, v_col0, mix, mix_shape, ctx=None):
    group = n_q // n_kv
    tq = _tile(seq, 512)
    nq = seq // tq
    rb_q = row0 // tq
    rb_s = row0 // seq
    in_specs = [
        pl.BlockSpec((tq, group * hd), lambda b, g, i: (rb_q + b * nq + i, g)),
        pl.BlockSpec((seq, hd), lambda b, g, i: (rb_s + b, n_q + g)),
        pl.BlockSpec((seq, hd), lambda b, g, i: (rb_s + b, v_col0 // hd + g)),
    ]
    args = [qk, qk, z]
    if ctx is not None:
        cache_k, cache_v, layer = ctx
        past = cache_k.shape[2]
        c_spec = pl.BlockSpec((None, None, past, hd), lambda b, g, i: (b, layer, 0, g))
        in_specs += [c_spec, c_spec]
        args += [cache_k.reshape(*cache_k.shape[:3], n_kv * hd), cache_v.reshape(*cache_v.shape[:3], n_kv * hd)]
    out_shape, aliases = _mix_target(mix, mix_shape, args, in_specs)
    return pl.pallas_call(
        functools.partial(_attn_kernel, group=group, hd=hd, has_ctx=ctx is not None),
        grid=(n_batch, n_kv, nq),
        in_specs=in_specs,
        out_specs=pl.BlockSpec((tq, group * hd), lambda b, g, i: (rb_q + b * nq + i, g)),
        out_shape=out_shape,
        input_output_aliases=aliases,
        compiler_params=_params("arbitrary", "arbitrary", "arbitrary"),
        name="attn_ctx" if ctx is not None else "attn",
    )(*args)


def _gla_kernel(q_ref, k_ref, v_ref, g_ref, cf_ref, cb_ref, nw_ref, *rest, seq, chunk, has_s0, want_state, scale,
                n_aliased):
    rest = list(rest)
    s0_ref = rest.pop(0) if has_s0 else None
    del rest[:n_aliased]
    o_ref = rest.pop(0)
    sout_ref = rest.pop(0) if want_state else None
    qdf_ref, qdb_ref, oin_ref, kvf_ref, kvb_ref, spf_ref, spb_ref = rest
    n = seq // chunk
    unroll = True if n <= 8 else 8

    row = lax.broadcasted_iota(jnp.int32, (chunk, chunk), 0)
    col = lax.broadcasted_iota(jnp.int32, (chunk, chunk), 1)
    lower = col <= row
    upper = col >= row

    def rows(i):
        return pl.ds(pl.multiple_of(i * chunk, chunk), chunk)

    def intra(i, cum_ref, mask, tot_row, qd_ref, kv_ref):
        sl = rows(i)
        cum = cum_ref[sl, :]
        tot = cum[tot_row:tot_row + 1, :]
        q = q_ref[sl, :].astype(F32) * scale
        k = k_ref[sl, :].astype(F32)
        v = v_ref[sl, :]
        q_dec = (q * jnp.exp(cum)).astype(BF16)
        k_inv = (k * jnp.exp(-cum)).astype(BF16)
        k_end = (k * jnp.exp(tot - cum)).astype(BF16)
        qd_ref[sl, :] = q_dec
        kv_ref[i] = _dot_tn(v, k_end)
        att = jnp.where(mask, _dot_nt(q_dec, k_inv), 0.0).astype(BF16)
        return _dot(att, v)

    def phase1(i, carry):
        oin_ref[rows(i), :] = (intra(i, cf_ref, lower, chunk - 1, qdf_ref, kvf_ref)
                               + intra(i, cb_ref, upper, 0, qdb_ref, kvb_ref))
        return carry

    lax.fori_loop(0, n, phase1, 0, unroll=unroll)

    def scan(reverse, cum_ref, tot_row, kv_ref, sp_ref, init):
        def step(j, st):
            i = n - 1 - j if reverse else j
            tot = cum_ref[pl.ds(i * chunk + tot_row, 1), :]
            sp_ref[i] = st.astype(BF16)
            return jnp.exp(tot) * st + kv_ref[i]

        return lax.fori_loop(0, n, step, init)

    zero = jnp.zeros(kvf_ref.shape[1:], F32)
    st_f = scan(False, cf_ref, chunk - 1, kvf_ref, spf_ref, s0_ref[0].T if has_s0 else zero)
    st_b = scan(True, cb_ref, 0, kvb_ref, spb_ref, s0_ref[1].T if has_s0 else zero)
    if want_state:
        sout_ref[0] = st_f.T
        sout_ref[1] = st_b.T

    def phase3(i, carry):
        sl = rows(i)
        o = oin_ref[sl, :] + _dot_nt(qdf_ref[sl, :], spf_ref[i]) + _dot_nt(qdb_ref[sl, :], spb_ref[i])
        ms = jnp.mean(o * o, axis=-1, keepdims=True)
        y = o * lax.rsqrt(ms + EPS) * nw_ref[...]
        o_ref[sl, :] = (y * _silu(g_ref[sl, :].astype(F32))).astype(o_ref.dtype)
        return carry

    lax.fori_loop(0, n, phase3, 0, unroll=unroll)


def gla(z, la, norm_w, row0, n_batch, seq, mix, mix_col0, *, heads, dk, dv, q_col0, k_col0, v_col0, g_col0, s0=None,
        new_state=None):
    rb = row0 // seq
    n_chunks = seq // GLA_CHUNK
    want_state = new_state is not None
    in_specs = [
        pl.BlockSpec((seq, dk), lambda b, hh: (rb + b, q_col0 // dk + hh)),
        pl.BlockSpec((seq, dk), lambda b, hh: (rb + b, k_col0 // dk + hh)),
        pl.BlockSpec((seq, dv), lambda b, hh: (rb + b, v_col0 // dv + hh)),
        pl.BlockSpec((seq, dv), lambda b, hh: (rb + b, g_col0 // dv + hh)),
        pl.BlockSpec((seq, dk), lambda b, hh: (rb + b, hh)),
        pl.BlockSpec((seq, dk), lambda b, hh: (rb + b, heads + hh)),
        pl.BlockSpec((1, dv), lambda b, hh: (0, 0)),
    ]
    args = [z, z, z, z, la, la, norm_w.reshape(1, dv)]
    if s0 is not None:
        state, layer = s0
        in_specs += [pl.BlockSpec((None, None, 2, None, dk, dv), lambda b, hh: (b, layer, 0, hh, 0, 0))]
        args += [state]
    n_fixed = len(args)
    mix_shape, aliases = _mix_target(mix, None, args, in_specs)
    out_specs = [pl.BlockSpec((seq, dv), lambda b, hh: (rb + b, mix_col0 // dv + hh))]
    out_shape = [mix_shape]
    if want_state:
        states, layer_out, depth = new_state
        out_specs += [pl.BlockSpec((None, None, 2, None, dk, dv), lambda b, hh: (b, layer_out, 0, hh, 0, 0))]
        out_shape += [jax.ShapeDtypeStruct((n_batch, depth, 2, heads, dk, dv), F32)]
        if states is not None:
            args.append(states)
            in_specs.append(pl.BlockSpec(memory_space=pl.ANY))
            aliases[len(args) - 1] = 1
    res = pl.pallas_call(
        functools.partial(_gla_kernel, seq=seq, chunk=GLA_CHUNK, has_s0=s0 is not None, want_state=want_state,
                          scale=dk ** -0.5, n_aliased=len(args) - n_fixed),
        grid=(n_batch, heads),
        in_specs=in_specs,
        out_specs=out_specs,
        out_shape=out_shape,
        input_output_aliases=aliases,
        scratch_shapes=[pltpu.VMEM((seq, dk), BF16), pltpu.VMEM((seq, dk), BF16), pltpu.VMEM((seq, dv), F32),
                        pltpu.VMEM((n_chunks, dv, dk), F32), pltpu.VMEM((n_chunks, dv, dk), F32),
                        pltpu.VMEM((n_chunks, dv, dk), BF16), pltpu.VMEM((n_chunks, dv, dk), BF16)],
        compiler_params=_params("arbitrary", "arbitrary"),
        name="gla_state" if want_state else "gla",
    )(*args)
    return res if want_state else (res[0], None)


def _rope_tables(seq, hd, lead):
    axis_dim = hd // 2
    t = jnp.arange(seq)
    rowp = (t // GRID_W).astype(F32)
    colp = (t % GRID_W).astype(F32)
    inv = ROPE_THETA ** (-jnp.arange(axis_dim // 2, dtype=F32) * 2.0 / axis_dim)
    ang_r = rowp[:, None] * inv
    ang_c = colp[:, None] * inv
    cos = jnp.concatenate([jnp.cos(ang_r), jnp.cos(ang_r), jnp.cos(ang_c), jnp.cos(ang_c)], axis=-1)
    sin = jnp.concatenate([-jnp.sin(ang_r), jnp.sin(ang_r), -jnp.sin(ang_c), jnp.sin(ang_c)], axis=-1)
    cos = jnp.concatenate([jnp.ones((lead, hd), F32), cos], axis=0)
    sin = jnp.concatenate([jnp.zeros((lead, hd), F32), sin], axis=0)
    return cos, sin


def kernel(x_prompt, x_sample, cache_k, cache_v, state_gla, c, c_ctx, norm1_w, norm2_w, w_ada, b_ada, w_in,
           q_norm_w, k_norm_w, gla_up, gla_bias, gla_norm_w, w_out, ffn_w1, ffn_w3, ffn_w2, router_w, moe_w1,
           moe_w3, moe_w2):
    batch, seq1, d = x_prompt.shape
    dec_batch, seq2, _ = x_sample.shape
    depth = w_in.shape[0]
    hd = q_norm_w.shape[-1]
    n_kv = cache_k.shape[3]
    n_q = d // 2 // hd
    heads, dk, dv = state_gla.shape[3:]
    rank = gla_up.shape[2]
    n_experts = router_w.shape[-1]
    attn_w, kv_w, key_w, gla_w = n_q * hd, n_kv * hd, heads * dk, heads * dv
    main_cols = attn_w + 2 * kv_w + 2 * key_w + 2 * gla_w
    q_col0 = attn_w + 2 * kv_w
    k_col0 = q_col0 + key_w
    v_col0 = k_col0 + key_w
    g_col0 = v_col0 + gla_w
    n1, n2 = batch * seq1, dec_batch * seq2
    n = n1 + n2
    assert hd == LANES and n1 % seq2 == 0 and 2 * rank <= LANES and n_experts <= LANES

    def row_group(m, tm):
        tok = m * tm
        return jnp.where(tok < n1, 0, 1 + (tok - n1) // seq2)

    def table_block(m, tm):
        tok = m * tm
        return jnp.where(tok < n1, 0, 1 + ((tok - n1) % seq2) // tm)

    n_rows = -(-(1 + dec_batch) // SUBLANES) * SUBLANES
    cond = jnp.zeros((n_rows, d), F32).at[0].set(c_ctx).at[1:1 + dec_batch].set(c)
    mod_all = ada_modulation(cond, w_ada, b_ada)

    tm_row = _tile(math.gcd(n1, seq2), 256)
    tm_prep = _tile(math.gcd(n1, seq2), 512)
    m_cap = math.gcd(n1, seq2)
    cos, sin = _rope_tables(seq2, hd, tm_prep)

    w_in_t = jnp.swapaxes(w_in, 1, 2)
    h = jnp.concatenate([x_prompt.reshape(n1, d), x_sample.reshape(n2, d)], axis=0)
    new_k, new_v, new_s = [], [], None
    for l in range(depth):
        mod = mod_all[l].reshape(n_rows, 1, N_MOD * d)
        w_lr = jnp.zeros((d, LANES), F32).at[:, :2 * rank].set(w_in_t[l, main_cols:, :].T)
        up = jnp.zeros((LANES, 2 * key_w), F32)
        up = up.at[:rank, :key_w].set(gla_up[l, 0]).at[rank:2 * rank, key_w:].set(gla_up[l, 1])
        bias = gla_bias[l].reshape(1, 2 * key_w)

        u, la = norm_modulate(h, norm1_w[l], mod, 1, 0, row_group, tm_row, mode="decay", extra=(w_lr, up, bias))
        z = matmul(u, [w_in_t], l, mode="plain", out_dtype=BF16, m_cap=m_cap, n_cols=main_cols, w_transposed=True)

        qk, kn = qk_prepare(z, q_norm_w[l], k_norm_w[l], cos, sin, table_block, tm_prep, n_q=n_q, n_kv=n_kv, hd=hd)
        new_k.append(kn[:n1].reshape(batch, seq1, n_kv, hd))
        new_v.append(z[:n1, attn_w + kv_w:attn_w + 2 * kv_w].astype(F32).reshape(batch, seq1, n_kv, hd))

        attn_kw = dict(n_q=n_q, n_kv=n_kv, hd=hd, v_col0=attn_w + kv_w, mix_shape=(n, attn_w + gla_w))
        mix = attention(qk, z, 0, batch, seq1, mix=None, **attn_kw)
        mix = attention(qk, z, n1, dec_batch, seq2, mix=mix, ctx=(cache_k, cache_v, l), **attn_kw)
        gla_kw = dict(heads=heads, dk=dk, dv=dv, q_col0=q_col0, k_col0=k_col0, v_col0=v_col0, g_col0=g_col0)
        mix, new_s = gla(z, la, gla_norm_w[l], 0, batch, seq1, mix, attn_w, new_state=(new_s, l, depth), **gla_kw)
        mix, _ = gla(z, la, gla_norm_w[l], n1, dec_batch, seq2, mix, attn_w, s0=(state_gla, l), **gla_kw)

        h = matmul(mix, [w_out], l, mode="resid", out_dtype=F32, m_cap=m_cap, h=h, mod=mod, gate_col=2 * d,
                   row_group=row_group)

        i = l // 2
        if l % 2 == 0:
            (u2,) = norm_modulate(h, norm2_w[l], mod, 4, 3, row_group, tm_row, mode="plain", extra=())
            hid = matmul(u2, [ffn_w1, ffn_w3], i, mode="swiglu", out_dtype=BF16, m_cap=m_cap)
            h = matmul(hid, [ffn_w2], i, mode="resid", out_dtype=F32, m_cap=m_cap, h=h, mod=mod, gate_col=5 * d,
                       row_group=row_group)
        else:
            rw = jnp.zeros((d, LANES), F32).at[:, :n_experts].set(router_w[i])
            xp, meta, counts = norm_modulate(h, norm2_w[l], mod, 4, 3, row_group, tm_row, mode="router",
                                             extra=(rw,), n_experts=n_experts)
            h = moe_ffn(xp, meta, counts, h, mod, 5 * d, row_group, moe_w1, moe_w3, moe_w2, i, tm_row, tm_prep)

    y_prompt = h[:n1].reshape(batch, seq1, d)
    y_sample = h[n1:].reshape(dec_batch, seq2, d)
    return (y_prompt, y_sample, jnp.stack(new_k, axis=1), jnp.stack(new_v, axis=1),
            new_s.astype(x_prompt.dtype))
```

```python
import functools
import math

import jax
import jax.numpy as jnp
from jax import lax
from jax.experimental import pallas as pl
from jax.experimental.pallas import tpu as pltpu

F32 = jnp.float32
BF16 = jnp.bfloat16

EPS = 1e-6
GRID_W = 64
GLA_CHUNK = 64
GLA_TAU = 16.0
ROPE_THETA = 10000.0
N_MOD = 6
LANES = 128
SUBLANES = 8
VMEM_LIMIT_BYTES = 56 * 1024 * 1024


def _params(*semantics):
    return pltpu.CompilerParams(dimension_semantics=semantics, vmem_limit_bytes=VMEM_LIMIT_BYTES)


def _tile(n, cap):
    t = cap
    while t > 1 and n % t:
        t //= 2
    return t


def _dot(a, b):
    return jnp.dot(a, b, preferred_element_type=F32)


def _dot_nt(a, b):
    return lax.dot_general(a, b, (((1,), (1,)), ((), ())), preferred_element_type=F32)


def _dot_tn(a, b):
    return lax.dot_general(a, b, (((0,), (0,)), ((), ())), preferred_element_type=F32)


def _split_bf16(x):
    hi = x.astype(BF16)
    lo = (x - hi.astype(F32)).astype(BF16)
    return hi, lo


def _silu(x):
    return x * jax.nn.sigmoid(x)


def _ada_kernel(c_ref, w_ref, b_ref, o_ref):
    s = _silu(c_ref[...]).astype(BF16)
    o_ref[...] = _dot(s, w_ref[...].astype(BF16)) + b_ref[...]


def ada_modulation(cond, w_ada, b_ada):
    depth, d, n = w_ada.shape
    r = cond.shape[0]
    tn = _tile(n, 1024)
    return pl.pallas_call(
        _ada_kernel,
        grid=(depth, n // tn),
        in_specs=[
            pl.BlockSpec((r, d), lambda l, j: (0, 0)),
            pl.BlockSpec((None, d, tn), lambda l, j: (l, 0, j)),
            pl.BlockSpec((None, 1, tn), lambda l, j: (l, 0, j)),
        ],
        out_specs=pl.BlockSpec((None, r, tn), lambda l, j: (l, 0, j)),
        out_shape=jax.ShapeDtypeStruct((depth, r, n), F32),
        compiler_params=_params("arbitrary", "arbitrary"),
        name="ada",
    )(cond, w_ada, b_ada.reshape(depth, 1, n))


def _norm_kernel(h_ref, nw_ref, sc_ref, sh_ref, *rest, mode, n_experts):
    x = h_ref[...]
    ms = jnp.mean(x * x, axis=-1, keepdims=True)
    y = x * lax.rsqrt(ms + EPS) * nw_ref[...]
    u = y * (1.0 + sc_ref[...]) + sh_ref[...]
    if mode == "decay":
        wlr_ref, up_ref, bias_ref, u_ref, la_ref = rest
        ub = u.astype(BF16)
        u_ref[...] = ub
        lr = _dot(ub, wlr_ref[...].astype(BF16))
        lr_hi, lr_lo = _split_bf16(lr)
        up_hi, up_lo = _split_bf16(up_ref[...])
        logits = _dot(lr_hi, up_hi) + (_dot(lr_lo, up_hi) + _dot(lr_hi, up_lo)) + bias_ref[...]
        log_sig = jnp.minimum(logits, 0.0) - jnp.log1p(jnp.exp(-jnp.abs(logits)))
        la = log_sig * (1.0 / GLA_TAU)
        tm = la.shape[0]
        half = la.shape[1] // 2
        t_row = lax.broadcasted_iota(jnp.int32, (tm, tm), 0)
        t_col = lax.broadcasted_iota(jnp.int32, (tm, tm), 1)
        same_chunk = (t_row // GLA_CHUNK) == (t_col // GLA_CHUNK)
        tri_f = jnp.logical_and(same_chunk, t_col <= t_row).astype(BF16)
        tri_b = jnp.logical_and(same_chunk, t_col >= t_row).astype(BF16)
        la_hi, la_lo = _split_bf16(la)
        la_ref[:, :half] = _dot(tri_f, la_hi[:, :half]) + _dot(tri_f, la_lo[:, :half])
        la_ref[:, half:] = _dot(tri_b, la_hi[:, half:]) + _dot(tri_b, la_lo[:, half:])
    elif mode == "plain":
        (u_ref,) = rest
        u_ref[...] = u.astype(BF16)
    else:
        rw_ref, u_ref, meta_ref, cnt_ref = rest
        u_ref[...] = u.astype(BF16)
        u_hi, u_lo = _split_bf16(u)
        rw_hi, rw_lo = _split_bf16(rw_ref[...])
        logits = _dot(u_hi, rw_hi) + (_dot(u_lo, rw_hi) + _dot(u_hi, rw_lo))
        lane = lax.broadcasted_iota(jnp.int32, logits.shape, 1)
        neg = jnp.float32(-jnp.inf)
        logits = jnp.where(lane < n_experts, logits, neg)
        v1 = jnp.max(logits, axis=-1, keepdims=True)
        i1 = jnp.min(jnp.where(logits == v1, lane, LANES), axis=-1, keepdims=True)
        rest_l = jnp.where(lane == i1, neg, logits)
        v2 = jnp.max(rest_l, axis=-1, keepdims=True)
        i2 = jnp.min(jnp.where(rest_l == v2, lane, LANES), axis=-1, keepdims=True)
        e2 = jnp.exp(v2 - v1)
        g1 = 1.0 / (1.0 + e2)
        g2 = e2 / (1.0 + e2)

        tm = logits.shape[0]
        chosen = jnp.logical_or(lane == i1, lane == i2)
        t_row = lax.broadcasted_iota(jnp.int32, (tm, tm), 0)
        t_col = lax.broadcasted_iota(jnp.int32, (tm, tm), 1)
        before = _dot((t_col < t_row).astype(BF16), chosen.astype(BF16))
        r1 = jnp.sum(jnp.where(lane == i1, before, 0.0), axis=-1, keepdims=True)
        r2 = jnp.sum(jnp.where(lane == i2, before, 0.0), axis=-1, keepdims=True)
        cnt_ref[...] = jnp.broadcast_to(jnp.sum(chosen.astype(F32), axis=0, keepdims=True), cnt_ref.shape)
        fields = (i1.astype(F32), i2.astype(F32), g1, g2, r1, r2)
        meta = jnp.zeros(logits.shape, F32)
        for pos, val in enumerate(fields):
            meta = jnp.where(lane == pos, val, meta)
        meta_ref[...] = meta


def norm_modulate(h, norm_w, mod, sc_col, sh_col, row_group, tm, *, mode, extra, n_experts=0):
    n, d = h.shape
    in_specs = [
        pl.BlockSpec((tm, d), lambda m: (m, 0)),
        pl.BlockSpec((1, d), lambda m: (0, 0)),
        pl.BlockSpec((None, 1, d), lambda m: (row_group(m, tm), 0, sc_col)),
        pl.BlockSpec((None, 1, d), lambda m: (row_group(m, tm), 0, sh_col)),
    ]
    if mode == "decay":
        wlr, up, bias = extra
        in_specs += [
            pl.BlockSpec(wlr.shape, lambda m: (0, 0)),
            pl.BlockSpec(up.shape, lambda m: (0, 0)),
            pl.BlockSpec(bias.shape, lambda m: (0, 0)),
        ]
        side = up.shape[1]
    elif mode == "router":
        (rw,) = extra
        in_specs += [pl.BlockSpec(rw.shape, lambda m: (0, 0))]
    if mode == "router":
        out_specs = [pl.BlockSpec((tm, d), lambda m: (m, 0)),
                     pl.BlockSpec((tm, LANES), lambda m: (m, 0)),
                     pl.BlockSpec((SUBLANES, LANES), lambda m: (m, 0))]
        out_shape = [jax.ShapeDtypeStruct((n, d), BF16), jax.ShapeDtypeStruct((n, LANES), F32),
                     jax.ShapeDtypeStruct((n // tm * SUBLANES, LANES), F32)]
    else:
        out_specs = [pl.BlockSpec((tm, d), lambda m: (m, 0))]
        out_shape = [jax.ShapeDtypeStruct((n, d), BF16)]
        if mode == "decay":
            out_specs += [pl.BlockSpec((tm, side), lambda m: (m, 0))]
            out_shape += [jax.ShapeDtypeStruct((n, side), F32)]
    return pl.pallas_call(
        functools.partial(_norm_kernel, mode=mode, n_experts=n_experts),
        grid=(n // tm,),
        in_specs=in_specs,
        out_specs=out_specs,
        out_shape=out_shape,
        compiler_params=_params("arbitrary"),
        name="norm_" + mode,
    )(h, norm_w.reshape(1, d), mod, mod, *extra)


def _mm_kernel(*refs, mode, n_w, w_transposed):
    mm = _dot_nt if w_transposed else _dot
    x_ref = refs[0]
    w_refs = refs[1:1 + n_w]
    wb_refs = refs[len(refs) - n_w:]
    rest = refs[1 + n_w:len(refs) - n_w]

    @pl.when(pl.program_id(1) == 0)
    def _():
        for w_ref, wb_ref in zip(w_refs, wb_refs):
            wb_ref[...] = w_ref[...].astype(BF16)

    x = x_ref[...]
    if mode == "swiglu":
        (o_ref,) = rest
        o_ref[...] = (_silu(mm(x, wb_refs[0][...])) * mm(x, wb_refs[1][...])).astype(o_ref.dtype)
    elif mode == "resid":
        h_ref, g_ref, o_ref = rest
        o_ref[...] = h_ref[...] + g_ref[...] * mm(x, wb_refs[0][...])
    else:
        (o_ref,) = rest
        o_ref[...] = mm(x, wb_refs[0][...]).astype(o_ref.dtype)


def _mm_tiles(m_cap, k, n_cols, n_w, mode):
    tn = _tile(n_cols, 512 if (n_w > 1 or k > 4096) else 1024)
    tm = _tile(m_cap, 512 if (mode == "resid" or k > 4096) else 1024)
    io_bytes = {"plain": 2, "swiglu": 2, "resid": 8}[mode]
    est = n_w * k * tn * (4 + 2) + 2 * tm * k * 2 + 2 * tm * tn * io_bytes + tm * tn * 4 * n_w
    assert est <= VMEM_LIMIT_BYTES, (est, tm, tn)
    return tm, tn


def matmul(x, ws, layer, *, mode, out_dtype, m_cap, n_cols=None, w_transposed=False, h=None, mod=None,
           gate_col=None, row_group=None):
    m_tot, k = x.shape
    n_cols = ws[0].shape[1 if w_transposed else 2] if n_cols is None else n_cols
    tm, tn = _mm_tiles(m_cap, k, n_cols, len(ws), mode)
    if w_transposed:
        w_spec = pl.BlockSpec((None, tn, k), lambda n, m: (layer, n, 0), pipeline_mode=pl.Buffered(1))
    else:
        w_spec = pl.BlockSpec((None, k, tn), lambda n, m: (layer, 0, n), pipeline_mode=pl.Buffered(1))
    in_specs = [pl.BlockSpec((tm, k), lambda n, m: (m, 0))] + [w_spec] * len(ws)
    args = [x, *ws]
    if mode == "resid":
        gate_blk = gate_col // tn
        in_specs += [
            pl.BlockSpec((tm, tn), lambda n, m: (m, n)),
            pl.BlockSpec((None, 1, tn), lambda n, m: (row_group(m, tm), 0, gate_blk + n)),
        ]
        args += [h, mod]
    return pl.pallas_call(
        functools.partial(_mm_kernel, mode=mode, n_w=len(ws), w_transposed=w_transposed),
        grid=(n_cols // tn, m_tot // tm),
        in_specs=in_specs,
        out_specs=pl.BlockSpec((tm, tn), lambda n, m: (m, n)),
        out_shape=jax.ShapeDtypeStruct((m_tot, n_cols), out_dtype),
        scratch_shapes=[pltpu.VMEM((tn, k) if w_transposed else (k, tn), BF16)] * len(ws),
        compiler_params=_params("arbitrary", "arbitrary"),
        name="mm_" + mode,
    )(*args)


SEG_ALIGN = 16


def _run_copies(t, n_experts, lo_ref, g_ref, len_ref, make_copy, wait):
    for e in range(n_experts):
        base = t * n_experts + e

        def body(c, carry, base=base):
            lo = pl.multiple_of(lo_ref[base] + c * SEG_ALIGN, SEG_ALIGN)
            g = pl.multiple_of(g_ref[base] + c * SEG_ALIGN, SEG_ALIGN)
            cp = make_copy(lo, g)
            if wait:
                cp.wait()
            else:
                cp.start()
            return carry

        lax.fori_loop(0, len_ref[base] // SEG_ALIGN, body, 0)


def _dispatch_kernel(lo_ref, g_ref, len_ref, u_ref, meta_ref, *rest, n_experts):
    xs_hbm, sorted_ref, sem = rest[-3:]
    meta = meta_ref[...]
    lane = lax.broadcasted_iota(jnp.int32, meta.shape, 1)
    d1 = jnp.sum(jnp.where(lane == 6, meta, 0.0), axis=-1, keepdims=True).astype(jnp.int32)
    d2 = jnp.sum(jnp.where(lane == 7, meta, 0.0), axis=-1, keepdims=True).astype(jnp.int32)
    slot = lax.broadcasted_iota(jnp.int32, (meta.shape[0], sorted_ref.shape[0]), 1)
    onehot = jnp.logical_or(slot == d1, slot == d2).astype(BF16)
    sorted_ref[...] = _dot_tn(onehot, u_ref[...]).astype(BF16)

    def make_copy(lo, g):
        return pltpu.make_async_copy(sorted_ref.at[pl.ds(lo, SEG_ALIGN)], xs_hbm.at[pl.ds(g, SEG_ALIGN)], sem)

    for wait in (False, True):
        _run_copies(pl.program_id(0), n_experts, lo_ref, g_ref, len_ref, make_copy, wait)


def dispatch_rows(u, meta, seg_lo, seg_g, seg_len, buf, tm, n_experts, r_loc):
    n, d = u.shape
    p_rows = buf.shape[0]
    return pl.pallas_call(
        functools.partial(_dispatch_kernel, n_experts=n_experts),
        grid_spec=pltpu.PrefetchScalarGridSpec(
            num_scalar_prefetch=3,
            grid=(n // tm,),
            in_specs=[
                pl.BlockSpec((tm, d), lambda t, lo, g, ln: (t, 0)),
                pl.BlockSpec((tm, LANES), lambda t, lo, g, ln: (t, 0)),
                pl.BlockSpec(memory_space=pl.ANY),
            ],
            out_specs=pl.BlockSpec(memory_space=pl.ANY),
            scratch_shapes=[pltpu.VMEM((r_loc, d), BF16), pltpu.SemaphoreType.DMA(())],
        ),
        out_shape=jax.ShapeDtypeStruct((p_rows, d), BF16),
        input_output_aliases={5: 0},
        compiler_params=_params("arbitrary"),
        name="moe_dispatch",
    )(seg_lo, seg_g, seg_len, u, meta, buf)


def _expert_mm_kernel(te_ref, na_ref, new_ref, x_ref, *rest, n_w):
    w_refs, o_ref, wb_refs = rest[:n_w], rest[n_w], rest[n_w + 1:]
    i = pl.program_id(1)

    @pl.when(new_ref[i] == 1)
    def _():
        for w_ref, wb_ref in zip(w_refs, wb_refs):
            wb_ref[...] = w_ref[...].astype(BF16)

    @pl.when(i < na_ref[0])
    def _():
        x = x_ref[...]
        if n_w == 2:
            o_ref[...] = (_silu(_dot(x, wb_refs[0][...])) * _dot(x, wb_refs[1][...])).astype(o_ref.dtype)
        else:
            o_ref[...] = _dot(x, wb_refs[0][...]).astype(o_ref.dtype)

    @pl.when(i >= na_ref[0])
    def _():
        o_ref[...] = jnp.zeros_like(o_ref)


def _expert_matmul(x, ws, layer, tn, tile_expert, n_active, new_expert, tm, name):
    p_rows, k = x.shape
    n_cols = ws[0].shape[3]
    w_spec = pl.BlockSpec((None, None, k, tn), lambda j, i, te, na, new: (layer, te[i], 0, j))
    return pl.pallas_call(
        functools.partial(_expert_mm_kernel, n_w=len(ws)),
        grid_spec=pltpu.PrefetchScalarGridSpec(
            num_scalar_prefetch=3,
            grid=(n_cols // tn, p_rows // tm),
            in_specs=[pl.BlockSpec((tm, k), lambda j, i, te, na, new: (i, 0))] + [w_spec] * len(ws),
            out_specs=pl.BlockSpec((tm, tn), lambda j, i, te, na, new: (i, j)),
            scratch_shapes=[pltpu.VMEM((k, tn), BF16)] * len(ws),
        ),
        out_shape=jax.ShapeDtypeStruct((p_rows, n_cols), BF16),
        compiler_params=_params("arbitrary", "arbitrary"),
        name=name,
    )(tile_expert, n_active, new_expert, x, *ws)


def expert_ffn(xs, w1, w3, w2, layer, tile_expert, n_active, new_expert, tm):
    f, d = w2.shape[2:]
    route = (tile_expert, n_active, new_expert, tm)
    hid = _expert_matmul(xs, [w1, w3], layer, _tile(f, 512), *route, "expert_swiglu")
    return _expert_matmul(hid, [w2], layer, _tile(d, 2048), *route, "expert_down")


def _combine_kernel(lo_ref, g_ref, len_ref, ys_hbm, meta_ref, h_ref, gate_ref, o_ref, runs_ref, sem, *, n_experts):
    t = pl.program_id(0)
    buf = t % 2

    def copies(tile, b, wait):
        def make_copy(lo, g):
            return pltpu.make_async_copy(ys_hbm.at[pl.ds(g, SEG_ALIGN)], runs_ref.at[b, pl.ds(lo, SEG_ALIGN)],
                                         sem.at[b])

        _run_copies(tile, n_experts, lo_ref, g_ref, len_ref, make_copy, wait)

    @pl.when(t == 0)
    def _():
        runs_ref[...] = jnp.zeros_like(runs_ref)
        copies(t, buf, False)

    @pl.when(t + 1 < pl.num_programs(0))
    def _():
        copies(t + 1, 1 - buf, False)

    copies(t, buf, True)

    meta = meta_ref[...]
    lane = lax.broadcasted_iota(jnp.int32, meta.shape, 1)

    def field(k):
        return jnp.sum(jnp.where(lane == k, meta, 0.0), axis=-1, keepdims=True)

    slot = lax.broadcasted_iota(jnp.int32, (meta.shape[0], runs_ref.shape[1]), 1)
    runs = runs_ref[buf]
    y1 = _dot((slot == field(6).astype(jnp.int32)).astype(BF16), runs)
    y2 = _dot((slot == field(7).astype(jnp.int32)).astype(BF16), runs)
    o_ref[...] = h_ref[...] + gate_ref[...] * (field(2) * y1 + field(3) * y2)


def combine_experts(ys, meta, seg_lo, seg_g, seg_len, h, mod, gate_col, row_group, tm, n_experts, r_loc):
    n, d = h.shape
    return pl.pallas_call(
        functools.partial(_combine_kernel, n_experts=n_experts),
        grid_spec=pltpu.PrefetchScalarGridSpec(
            num_scalar_prefetch=3,
            grid=(n // tm,),
            in_specs=[
                pl.BlockSpec(memory_space=pl.ANY),
                pl.BlockSpec((tm, LANES), lambda t, lo, g, ln: (t, 0)),
                pl.BlockSpec((tm, d), lambda t, lo, g, ln: (t, 0)),
                pl.BlockSpec((None, 1, d), lambda t, lo, g, ln: (row_group(t, tm), 0, gate_col // d)),
            ],
            out_specs=pl.BlockSpec((tm, d), lambda t, lo, g, ln: (t, 0)),
            scratch_shapes=[pltpu.VMEM((2, r_loc, d), BF16), pltpu.SemaphoreType.DMA((2,))],
        ),
        out_shape=jax.ShapeDtypeStruct((n, d), F32),
        compiler_params=_params("arbitrary"),
        name="moe_combine",
    )(seg_lo, seg_g, seg_len, ys, meta, h, mod)


def moe_ffn(u, meta, counts, h, mod, gate_col, row_group, w1, w3, w2, layer, tm, tm_e, xs_buf):
    n, d = h.shape
    n_e = w1.shape[1]
    n_tt = n // tm
    i32 = jnp.int32
    experts = meta[:, 0:2].astype(i32)
    ranks = meta[:, 4:6].astype(i32)
    cnt = counts.reshape(n_tt, SUBLANES, LANES)[:, 0, :n_e].astype(i32)
    run_len = (cnt + SEG_ALIGN - 1) // SEG_ALIGN * SEG_ALIGN
    run_lo = jnp.cumsum(run_len, axis=1) - run_len
    region = (jnp.sum(run_len, axis=0) + tm_e - 1) // tm_e * tm_e
    region_end = jnp.cumsum(region)
    run_g = (region_end - region)[None, :] + jnp.cumsum(run_len, axis=0) - run_len
    onehot = experts[:, :, None] == jnp.arange(n_e)[None, None, :]
    lo_tok = jnp.repeat(run_lo, tm, axis=0)[:, None, :]
    dest = jnp.sum(jnp.where(onehot, lo_tok, 0), axis=-1) + ranks
    meta = lax.dynamic_update_slice(meta, dest.astype(F32), (0, 6))

    r_loc = 2 * tm + n_e * SEG_ALIGN
    p_rows = -(-(2 * n + n_tt * n_e * SEG_ALIGN + n_e * tm_e) // tm_e) * tm_e
    n_tiles = p_rows // tm_e
    n_active = (region_end[-1] // tm_e).reshape(1).astype(i32)
    tile_row = jnp.arange(n_tiles) * tm_e
    tile_expert = jnp.sum(tile_row[:, None] >= region_end[None, :], axis=1)
    last = jnp.sum(tile_expert * (jnp.arange(n_tiles) == n_active[0] - 1))
    tile_expert = jnp.where(jnp.arange(n_tiles) < n_active[0], tile_expert, last).astype(i32)
    new_expert = jnp.concatenate([jnp.ones((1,), i32), (tile_expert[1:] != tile_expert[:-1]).astype(i32)])

    seg = (run_lo.reshape(-1).astype(i32), run_g.reshape(-1).astype(i32), run_len.reshape(-1).astype(i32))
    buf = jnp.zeros((p_rows, d), BF16) if xs_buf is None else xs_buf
    xs = dispatch_rows(u, meta, *seg, buf, tm, n_e, r_loc)
    ys = expert_ffn(xs, w1, w3, w2, layer, tile_expert, n_active, new_expert, tm_e)
    return combine_experts(ys, meta, *seg, h, mod, gate_col, row_group, tm, n_e, r_loc), xs


def _qkprep_kernel(z_ref, qw_ref, kw_ref, cos_ref, sin_ref, qk_ref, kn_ref, *, n_q, n_kv, hd, scale):
    cos = cos_ref[...]
    sin = sin_ref[...]
    lane = lax.broadcasted_iota(jnp.int32, cos.shape, 1)
    quarter = hd // 4
    first = (lane % (2 * quarter)) < quarter
    for hh in range(n_q + n_kv):
        cols = slice(hh * hd, (hh + 1) * hd)
        x = z_ref[:, cols].astype(F32)
        ms = jnp.mean(x * x, axis=-1, keepdims=True)
        y = x * lax.rsqrt(ms + EPS) * (qw_ref[...] if hh < n_q else kw_ref[...])
        if hh >= n_q:
            kn_ref[:, (hh - n_q) * hd:(hh - n_q + 1) * hd] = y
        partner = jnp.where(first, pltpu.roll(y, hd - quarter, 1), pltpu.roll(y, quarter, 1))
        r = y * cos + partner * sin
        if hh < n_q:
            r = r * scale
        qk_ref[:, cols] = r.astype(BF16)


def qk_prepare(z, q_norm_w, k_norm_w, cos, sin, table_block, tm, *, n_q, n_kv, hd):
    n = z.shape[0]
    w = (n_q + n_kv) * hd
    return pl.pallas_call(
        functools.partial(_qkprep_kernel, n_q=n_q, n_kv=n_kv, hd=hd, scale=hd ** -0.5),
        grid=(n // tm,),
        in_specs=[
            pl.BlockSpec((tm, w), lambda m: (m, 0)),
            pl.BlockSpec((1, hd), lambda m: (0, 0)),
            pl.BlockSpec((1, hd), lambda m: (0, 0)),
            pl.BlockSpec((tm, hd), lambda m: (table_block(m, tm), 0)),
            pl.BlockSpec((tm, hd), lambda m: (table_block(m, tm), 0)),
        ],
        out_specs=[pl.BlockSpec((tm, w), lambda m: (m, 0)), pl.BlockSpec((tm, n_kv * hd), lambda m: (m, 0))],
        out_shape=[jax.ShapeDtypeStruct((n, w), BF16), jax.ShapeDtypeStruct((n, n_kv * hd), F32)],
        compiler_params=_params("arbitrary"),
        name="qk_prep",
    )(z, q_norm_w.reshape(1, hd), k_norm_w.reshape(1, hd), cos, sin)


def _attn_kernel(q_ref, k_ref, v_ref, *rest, group, hd, has_ctx):
    o_ref = rest[-1]
    if has_ctx:
        kc_ref, vc_ref = rest[:2]
        kc = kc_ref[...].astype(BF16)
        vc = vc_ref[...].astype(BF16)
    k = k_ref[...]
    v = v_ref[...]
    for gi in range(group):
        cols = slice(gi * hd, (gi + 1) * hd)
        q = q_ref[:, cols]
        s = _dot_nt(q, k)
        mx = jnp.max(s, axis=-1, keepdims=True)
        if has_ctx:
            sc = _dot_nt(q, kc)
            mx = jnp.maximum(mx, jnp.max(sc, axis=-1, keepdims=True))
        p = jnp.exp(s - mx)
        den = jnp.sum(p, axis=-1, keepdims=True)
        o = _dot(p.astype(BF16), v)
        if has_ctx:
            pc = jnp.exp(sc - mx)
            den = den + jnp.sum(pc, axis=-1, keepdims=True)
            o = o + _dot(pc.astype(BF16), vc)
        o_ref[:, cols] = (o * (1.0 / den)).astype(o_ref.dtype)


def _mix_target(mix, mix_shape, args, in_specs):
    if mix is None:
        return jax.ShapeDtypeStruct(mix_shape, BF16), {}
    args.append(mix)
    in_specs.append(pl.BlockSpec(memory_space=pl.ANY))
    return jax.ShapeDtypeStruct(mix.shape, mix.dtype), {len(args) - 1: 0}


def attention(qk, z, row0, n_batch, seq, *, n_q, n_kv, hd, v_col0, mix, mix_shape, ctx=None):
    group = n_q // n_kv
    tq = _tile(seq, 512)
    nq = seq // tq
    rb_q = row0 // tq
    rb_s = row0 // seq
    in_specs = [
        pl.BlockSpec((tq, group * hd), lambda b, g, i: (rb_q + b * nq + i, g)),
        pl.BlockSpec((seq, hd), lambda b, g, i: (rb_s + b, n_q + g)),
        pl.BlockSpec((seq, hd), lambda b, g, i: (rb_s + b, v_col0 // hd + g)),
    ]
    args = [qk, qk, z]
    if ctx is not None:
        cache_k, cache_v, layer = ctx
        past = cache_k.shape[2]
        c_spec = pl.BlockSpec((None, None, past, hd), lambda b, g, i: (b, layer, 0, g))
        in_specs += [c_spec, c_spec]
        args += [cache_k.reshape(*cache_k.shape[:3], n_kv * hd), cache_v.reshape(*cache_v.shape[:3], n_kv * hd)]
    out_shape, aliases = _mix_target(mix, mix_shape, args, in_specs)
    return pl.pallas_call(
        functools.partial(_attn_kernel, group=group, hd=hd, has_ctx=ctx is not None),
        grid=(n_batch, n_kv, nq),
        in_specs=in_specs,
        out_specs=pl.BlockSpec((tq, group * hd), lambda b, g, i: (rb_q + b * nq + i, g)),
        out_shape=out_shape,
        input_output_aliases=aliases,
        compiler_params=_params("arbitrary", "arbitrary", "arbitrary"),
        name="attn_ctx" if ctx is not None else "attn",
    )(*args)


def _gla_kernel(q_ref, k_ref, v_ref, g_ref, cf_ref, cb_ref, nw_ref, *rest, seq, chunk, hb, dk, dv, has_s0, want_state,
                scale, n_aliased):
    rest = list(rest)
    s0_ref = rest.pop(0) if has_s0 else None
    del rest[:n_aliased]
    o_ref = rest.pop(0)
    sout_ref = rest.pop(0) if want_state else None
    qdf_ref, qdb_ref, oin_ref, kvf_ref, kvb_ref, spf_ref, spb_ref = rest
    n = seq // chunk
    unroll = True if n * hb <= 8 else max(1, 8 // hb)

    row = lax.broadcasted_iota(jnp.int32, (chunk, chunk), 0)
    col = lax.broadcasted_iota(jnp.int32, (chunk, chunk), 1)
    lower = col <= row
    upper = col >= row

    def rows(i):
        return pl.ds(pl.multiple_of(i * chunk, chunk), chunk)

    def kcols(hh):
        return slice(hh * dk, (hh + 1) * dk)

    def vcols(hh):
        return slice(hh * dv, (hh + 1) * dv)

    def intra(i, hh, cum_ref, mask, tot_row, qd_ref, kv_ref):
        sl = rows(i)
        cum = cum_ref[sl, kcols(hh)]
        tot = cum[tot_row:tot_row + 1, :]
        q = q_ref[sl, kcols(hh)].astype(F32) * scale
        k = k_ref[sl, kcols(hh)].astype(F32)
        v = v_ref[sl, vcols(hh)]
        q_dec = (q * jnp.exp(cum)).astype(BF16)
        k_inv = (k * jnp.exp(-cum)).astype(BF16)
        k_end = (k * jnp.exp(tot - cum)).astype(BF16)
        qd_ref[sl, kcols(hh)] = q_dec
        kv_ref[i * hb + hh] = _dot_tn(v, k_end)
        att = jnp.where(mask, _dot_nt(q_dec, k_inv), 0.0).astype(BF16)
        return _dot(att, v)

    def phase1(i, carry):
        for hh in range(hb):
            oin_ref[rows(i), vcols(hh)] = (intra(i, hh, cf_ref, lower, chunk - 1, qdf_ref, kvf_ref)
                                           + intra(i, hh, cb_ref, upper, 0, qdb_ref, kvb_ref))
        return carry

    lax.fori_loop(0, n, phase1, 0, unroll=unroll)

    def scan(hh, reverse, cum_ref, tot_row, kv_ref, sp_ref, init):
        def step(j, st):
            i = n - 1 - j if reverse else j
            group = pl.multiple_of(i * chunk + tot_row // SUBLANES * SUBLANES, SUBLANES)
            tot = cum_ref[pl.ds(group, SUBLANES), kcols(hh)][tot_row % SUBLANES:tot_row % SUBLANES + 1, :]
            sp_ref[i * hb + hh] = st.astype(BF16)
            return jnp.exp(tot) * st + kv_ref[i * hb + hh]

        return lax.fori_loop(0, n, step, init)

    zero = jnp.zeros(kvf_ref.shape[1:], F32)
    for hh in range(hb):
        st_f = scan(hh, False, cf_ref, chunk - 1, kvf_ref, spf_ref, s0_ref[0, hh].T if has_s0 else zero)
        st_b = scan(hh, True, cb_ref, 0, kvb_ref, spb_ref, s0_ref[1, hh].T if has_s0 else zero)
        if want_state:
            sout_ref[0, hh] = st_f.T
            sout_ref[1, hh] = st_b.T

    def phase3(i, carry):
        sl = rows(i)
        for hh in range(hb):
            o = (oin_ref[sl, vcols(hh)] + _dot_nt(qdf_ref[sl, kcols(hh)], spf_ref[i * hb + hh])
                 + _dot_nt(qdb_ref[sl, kcols(hh)], spb_ref[i * hb + hh]))
            ms = jnp.mean(o * o, axis=-1, keepdims=True)
            y = o * lax.rsqrt(ms + EPS) * nw_ref[...]
            o_ref[sl, vcols(hh)] = (y * _silu(g_ref[sl, vcols(hh)].astype(F32))).astype(o_ref.dtype)
        return carry

    lax.fori_loop(0, n, phase3, 0, unroll=unroll)


def gla(z, la, norm_w, row0, n_batch, seq, mix, mix_col0, *, heads, dk, dv, q_col0, k_col0, v_col0, g_col0, s0=None,
        new_state=None):
    rb = row0 // seq
    n_chunks = seq // GLA_CHUNK
    want_state = new_state is not None
    hb = _tile(heads, max(1, 512 // seq))
    col_starts = ((q_col0, dk), (k_col0, dk), (v_col0, dv), (g_col0, dv), (mix_col0, dv))
    while any(c % (hb * w) for c, w in col_starts):
        hb //= 2
    kw, vw = hb * dk, hb * dv
    in_specs = [
        pl.BlockSpec((seq, kw), lambda b, hh: (rb + b, q_col0 // kw + hh)),
        pl.BlockSpec((seq, kw), lambda b, hh: (rb + b, k_col0 // kw + hh)),
        pl.BlockSpec((seq, vw), lambda b, hh: (rb + b, v_col0 // vw + hh)),
        pl.BlockSpec((seq, vw), lambda b, hh: (rb + b, g_col0 // vw + hh)),
        pl.BlockSpec((seq, kw), lambda b, hh: (rb + b, hh)),
        pl.BlockSpec((seq, kw), lambda b, hh: (rb + b, heads // hb + hh)),
        pl.BlockSpec((1, dv), lambda b, hh: (0, 0)),
    ]
    args = [z, z, z, z, la, la, norm_w.reshape(1, dv)]
    if s0 is not None:
        state, layer = s0
        in_specs += [pl.BlockSpec((None, None, 2, hb, dk, dv), lambda b, hh: (b, layer, 0, hh, 0, 0))]
        args += [state]
    n_fixed = len(args)
    mix_shape, aliases = _mix_target(mix, None, args, in_specs)
    out_specs = [pl.BlockSpec((seq, vw), lambda b, hh: (rb + b, mix_col0 // vw + hh))]
    out_shape = [mix_shape]
    if want_state:
        states, layer_out, depth = new_state
        out_specs += [pl.BlockSpec((None, None, 2, hb, dk, dv), lambda b, hh: (b, layer_out, 0, hh, 0, 0))]
        out_shape += [jax.ShapeDtypeStruct((n_batch, depth, 2, heads, dk, dv), F32)]
        if states is not None:
            args.append(states)
            in_specs.append(pl.BlockSpec(memory_space=pl.ANY))
            aliases[len(args) - 1] = 1
    res = pl.pallas_call(
        functools.partial(_gla_kernel, seq=seq, chunk=GLA_CHUNK, hb=hb, dk=dk, dv=dv, has_s0=s0 is not None,
                          want_state=want_state, scale=dk ** -0.5, n_aliased=len(args) - n_fixed),
        grid=(n_batch, heads // hb),
        in_specs=in_specs,
        out_specs=out_specs,
        out_shape=out_shape,
        input_output_aliases=aliases,
        scratch_shapes=[pltpu.VMEM((seq, kw), BF16), pltpu.VMEM((seq, kw), BF16), pltpu.VMEM((seq, vw), F32),
                        pltpu.VMEM((n_chunks * hb, dv, dk), F32), pltpu.VMEM((n_chunks * hb, dv, dk), F32),
                        pltpu.VMEM((n_chunks * hb, dv, dk), BF16), pltpu.VMEM((n_chunks * hb, dv, dk), BF16)],
        compiler_params=_params("arbitrary", "arbitrary"),
        name="gla_state" if want_state else "gla",
    )(*args)
    return res if want_state else (res[0], None)


def _rope_tables(seq, hd, lead):
    axis_dim = hd // 2
    t = jnp.arange(seq)
    rowp = (t // GRID_W).astype(F32)
    colp = (t % GRID_W).astype(F32)
    inv = ROPE_THETA ** (-jnp.arange(axis_dim // 2, dtype=F32) * 2.0 / axis_dim)
    ang_r = rowp[:, None] * inv
    ang_c = colp[:, None] * inv
    cos = jnp.concatenate([jnp.cos(ang_r), jnp.cos(ang_r), jnp.cos(ang_c), jnp.cos(ang_c)], axis=-1)
    sin = jnp.concatenate([-jnp.sin(ang_r), jnp.sin(ang_r), -jnp.sin(ang_c), jnp.sin(ang_c)], axis=-1)
    cos = jnp.concatenate([jnp.ones((lead, hd), F32), cos], axis=0)
    sin = jnp.concatenate([jnp.zeros((lead, hd), F32), sin], axis=0)
    return cos, sin


def kernel(x_prompt, x_sample, cache_k, cache_v, state_gla, c, c_ctx, norm1_w, norm2_w, w_ada, b_ada, w_in,
           q_norm_w, k_norm_w, gla_up, gla_bias, gla_norm_w, w_out, ffn_w1, ffn_w3, ffn_w2, router_w, moe_w1,
           moe_w3, moe_w2):
    batch, seq1, d = x_prompt.shape
    dec_batch, seq2, _ = x_sample.shape
    depth = w_in.shape[0]
    hd = q_norm_w.shape[-1]
    n_kv = cache_k.shape[3]
    n_q = d // 2 // hd
    heads, dk, dv = state_gla.shape[3:]
    rank = gla_up.shape[2]
    n_experts = router_w.shape[-1]
    attn_w, kv_w, key_w, gla_w = n_q * hd, n_kv * hd, heads * dk, heads * dv
    main_cols = attn_w + 2 * kv_w + 2 * key_w + 2 * gla_w
    q_col0 = attn_w + 2 * kv_w
    k_col0 = q_col0 + key_w
    v_col0 = k_col0 + key_w
    g_col0 = v_col0 + gla_w
    n1, n2 = batch * seq1, dec_batch * seq2
    n = n1 + n2
    assert hd == LANES and n1 % seq2 == 0 and 2 * rank <= LANES and n_experts <= LANES

    def row_group(m, tm):
        tok = m * tm
        return jnp.where(tok < n1, 0, 1 + (tok - n1) // seq2)

    def table_block(m, tm):
        tok = m * tm
        return jnp.where(tok < n1, 0, 1 + ((tok - n1) % seq2) // tm)

    n_rows = -(-(1 + dec_batch) // SUBLANES) * SUBLANES
    cond = jnp.zeros((n_rows, d), F32).at[0].set(c_ctx).at[1:1 + dec_batch].set(c)
    mod_all = ada_modulation(cond, w_ada, b_ada)

    tm_row = _tile(math.gcd(n1, seq2), 256)
    tm_prep = _tile(math.gcd(n1, seq2), 512)
    m_cap = math.gcd(n1, seq2)
    cos, sin = _rope_tables(seq2, hd, tm_prep)

    w_in_t = jnp.swapaxes(w_in, 1, 2)
    h = jnp.concatenate([x_prompt.reshape(n1, d), x_sample.reshape(n2, d)], axis=0)
    new_k, new_v, new_s, xs_buf = [], [], None, None
    for l in range(depth):
        mod = mod_all[l].reshape(n_rows, 1, N_MOD * d)
        w_lr = jnp.zeros((d, LANES), F32).at[:, :2 * rank].set(w_in_t[l, main_cols:, :].T)
        up = jnp.zeros((LANES, 2 * key_w), F32)
        up = up.at[:rank, :key_w].set(gla_up[l, 0]).at[rank:2 * rank, key_w:].set(gla_up[l, 1])
        bias = gla_bias[l].reshape(1, 2 * key_w)

        u, la = norm_modulate(h, norm1_w[l], mod, 1, 0, row_group, tm_row, mode="decay", extra=(w_lr, up, bias))
        z = matmul(u, [w_in_t], l, mode="plain", out_dtype=BF16, m_cap=m_cap, n_cols=main_cols, w_transposed=True)

        qk, kn = qk_prepare(z, q_norm_w[l], k_norm_w[l], cos, sin, table_block, tm_prep, n_q=n_q, n_kv=n_kv, hd=hd)
        new_k.append(kn[:n1].reshape(batch, seq1, n_kv, hd))
        new_v.append(z[:n1, attn_w + kv_w:attn_w + 2 * kv_w].astype(F32).reshape(batch, seq1, n_kv, hd))

        attn_kw = dict(n_q=n_q, n_kv=n_kv, hd=hd, v_col0=attn_w + kv_w, mix_shape=(n, attn_w + gla_w))
        mix = attention(qk, z, 0, batch, seq1, mix=None, **attn_kw)
        mix = attention(qk, z, n1, dec_batch, seq2, mix=mix, ctx=(cache_k, cache_v, l), **attn_kw)
        gla_kw = dict(heads=heads, dk=dk, dv=dv, q_col0=q_col0, k_col0=k_col0, v_col0=v_col0, g_col0=g_col0)
        mix, new_s = gla(z, la, gla_norm_w[l], 0, batch, seq1, mix, attn_w, new_state=(new_s, l, depth), **gla_kw)
        mix, _ = gla(z, la, gla_norm_w[l], n1, dec_batch, seq2, mix, attn_w, s0=(state_gla, l), **gla_kw)

        h = matmul(mix, [w_out], l, mode="resid", out_dtype=F32, m_cap=m_cap, h=h, mod=mod, gate_col=2 * d,
                   row_group=row_group)

        i = l // 2
        if l % 2 == 0:
            (u2,) = norm_modulate(h, norm2_w[l], mod, 4, 3, row_group, tm_row, mode="plain", extra=())
            hid = matmul(u2, [ffn_w1, ffn_w3], i, mode="swiglu", out_dtype=BF16, m_cap=m_cap)
            h = matmul(hid, [ffn_w2], i, mode="resid", out_dtype=F32, m_cap=m_cap, h=h, mod=mod, gate_col=5 * d,
                       row_group=row_group)
        else:
            rw = jnp.zeros((d, LANES), F32).at[:, :n_experts].set(router_w[i])
            xp, meta, counts = norm_modulate(h, norm2_w[l], mod, 4, 3, row_group, tm_row, mode="router",
                                             extra=(rw,), n_experts=n_experts)
            h, xs_buf = moe_ffn(xp, meta, counts, h, mod, 5 * d, row_group, moe_w1, moe_w3, moe_w2, i, tm_row,
                                tm_prep, xs_buf)

    y_prompt = h[:n1].reshape(batch, seq1, d)
    y_sample = h[n1:].reshape(dec_batch, seq2, d)
    return (y_prompt, y_sample, jnp.stack(new_k, axis=1), jnp.stack(new_v, axis=1),
            new_s.astype(x_prompt.dtype))
```

```python
import functools
import math

import jax
import jax.numpy as jnp
from jax import lax
from jax.experimental import pallas as pl
from jax.experimental.pallas import tpu as pltpu

F32 = jnp.float32
BF16 = jnp.bfloat16

EPS = 1e-6
GRID_W = 64
GLA_CHUNK = 64
GLA_TAU = 16.0
ROPE_THETA = 10000.0
N_MOD = 6
LANES = 128
SUBLANES = 8
VMEM_LIMIT_BYTES = 56 * 1024 * 1024


def _params(*semantics):
    return pltpu.CompilerParams(dimension_semantics=semantics, vmem_limit_bytes=VMEM_LIMIT_BYTES)


def _tile(n, cap):
    t = cap
    while t > 1 and n % t:
        t //= 2
    return t


def _dot(a, b):
    return jnp.dot(a, b, preferred_element_type=F32)


def _dot_nt(a, b):
    return lax.dot_general(a, b, (((1,), (1,)), ((), ())), preferred_element_type=F32)


def _dot_tn(a, b):
    return lax.dot_general(a, b, (((0,), (0,)), ((), ())), preferred_element_type=F32)


def _split_bf16(x):
    hi = x.astype(BF16)
    lo = (x - hi.astype(F32)).astype(BF16)
    return hi, lo


def _silu(x):
    return x * jax.nn.sigmoid(x)


def _ada_kernel(c_ref, w_ref, b_ref, o_ref):
    s = _silu(c_ref[...]).astype(BF16)
    o_ref[...] = _dot(s, w_ref[...].astype(BF16)) + b_ref[...]


def ada_modulation(cond, w_ada, b_ada):
    depth, d, n = w_ada.shape
    r = cond.shape[0]
    tn = _tile(n, 1024)
    return pl.pallas_call(
        _ada_kernel,
        grid=(depth, n // tn),
        in_specs=[
            pl.BlockSpec((r, d), lambda l, j: (0, 0)),
            pl.BlockSpec((None, d, tn), lambda l, j: (l, 0, j)),
            pl.BlockSpec((None, 1, tn), lambda l, j: (l, 0, j)),
        ],
        out_specs=pl.BlockSpec((None, r, tn), lambda l, j: (l, 0, j)),
        out_shape=jax.ShapeDtypeStruct((depth, r, n), F32),
        compiler_params=_params("arbitrary", "arbitrary"),
        name="ada",
    )(cond, w_ada, b_ada.reshape(depth, 1, n))


def _norm_kernel(h_ref, nw_ref, sc_ref, sh_ref, *rest, mode, n_experts):
    x = h_ref[...]
    ms = jnp.mean(x * x, axis=-1, keepdims=True)
    y = x * lax.rsqrt(ms + EPS) * nw_ref[...]
    u = y * (1.0 + sc_ref[...]) + sh_ref[...]
    if mode == "decay":
        wlr_ref, up_ref, bias_ref, u_ref, la_ref = rest
        ub = u.astype(BF16)
        u_ref[...] = ub
        lr = _dot(ub, wlr_ref[...].astype(BF16))
        lr_hi, lr_lo = _split_bf16(lr)
        up_hi, up_lo = _split_bf16(up_ref[...])
        logits = _dot(lr_hi, up_hi) + (_dot(lr_lo, up_hi) + _dot(lr_hi, up_lo)) + bias_ref[...]
        log_sig = jnp.minimum(logits, 0.0) - jnp.log1p(jnp.exp(-jnp.abs(logits)))
        la = log_sig * (1.0 / GLA_TAU)
        tm = la.shape[0]
        half = la.shape[1] // 2
        t_row = lax.broadcasted_iota(jnp.int32, (tm, tm), 0)
        t_col = lax.broadcasted_iota(jnp.int32, (tm, tm), 1)
        same_chunk = (t_row // GLA_CHUNK) == (t_col // GLA_CHUNK)
        tri_f = jnp.logical_and(same_chunk, t_col <= t_row).astype(BF16)
        tri_b = jnp.logical_and(same_chunk, t_col >= t_row).astype(BF16)
        la_hi, la_lo = _split_bf16(la)
        la_ref[:, :half] = _dot(tri_f, la_hi[:, :half]) + _dot(tri_f, la_lo[:, :half])
        la_ref[:, half:] = _dot(tri_b, la_hi[:, half:]) + _dot(tri_b, la_lo[:, half:])
    elif mode == "plain":
        (u_ref,) = rest
        u_ref[...] = u.astype(BF16)
    else:
        rw_ref, u_ref, meta_ref, cnt_ref = rest
        u_ref[...] = u.astype(BF16)
        u_hi, u_lo = _split_bf16(u)
        rw_hi, rw_lo = _split_bf16(rw_ref[...])
        logits = _dot(u_hi, rw_hi) + (_dot(u_lo, rw_hi) + _dot(u_hi, rw_lo))
        lane = lax.broadcasted_iota(jnp.int32, logits.shape, 1)
        neg = jnp.float32(-jnp.inf)
        logits = jnp.where(lane < n_experts, logits, neg)
        v1 = jnp.max(logits, axis=-1, keepdims=True)
        i1 = jnp.min(jnp.where(logits == v1, lane, LANES), axis=-1, keepdims=True)
        rest_l = jnp.where(lane == i1, neg, logits)
        v2 = jnp.max(rest_l, axis=-1, keepdims=True)
        i2 = jnp.min(jnp.where(rest_l == v2, lane, LANES), axis=-1, keepdims=True)
        e2 = jnp.exp(v2 - v1)
        g1 = 1.0 / (1.0 + e2)
        g2 = e2 / (1.0 + e2)

        tm = logits.shape[0]
        chosen = jnp.logical_or(lane == i1, lane == i2)
        t_row = lax.broadcasted_iota(jnp.int32, (tm, tm), 0)
        t_col = lax.broadcasted_iota(jnp.int32, (tm, tm), 1)
        before = _dot((t_col < t_row).astype(BF16), chosen.astype(BF16))
        r1 = jnp.sum(jnp.where(lane == i1, before, 0.0), axis=-1, keepdims=True)
        r2 = jnp.sum(jnp.where(lane == i2, before, 0.0), axis=-1, keepdims=True)
        cnt_ref[...] = jnp.broadcast_to(jnp.sum(chosen.astype(F32), axis=0, keepdims=True), cnt_ref.shape)
        fields = (i1.astype(F32), i2.astype(F32), g1, g2, r1, r2)
        meta = jnp.zeros(logits.shape, F32)
        for pos, val in enumerate(fields):
            meta = jnp.where(lane == pos, val, meta)
        meta_ref[...] = meta


def norm_modulate(h, norm_w, mod, sc_col, sh_col, row_group, tm, *, mode, extra, n_experts=0):
    n, d = h.shape
    in_specs = [
        pl.BlockSpec((tm, d), lambda m: (m, 0)),
        pl.BlockSpec((1, d), lambda m: (0, 0)),
        pl.BlockSpec((None, 1, d), lambda m: (row_group(m, tm), 0, sc_col)),
        pl.BlockSpec((None, 1, d), lambda m: (row_group(m, tm), 0, sh_col)),
    ]
    if mode == "decay":
        wlr, up, bias = extra
        in_specs += [
            pl.BlockSpec(wlr.shape, lambda m: (0, 0)),
            pl.BlockSpec(up.shape, lambda m: (0, 0)),
            pl.BlockSpec(bias.shape, lambda m: (0, 0)),
        ]
        side = up.shape[1]
    elif mode == "router":
        (rw,) = extra
        in_specs += [pl.BlockSpec(rw.shape, lambda m: (0, 0))]
    if mode == "router":
        out_specs = [pl.BlockSpec((tm, d), lambda m: (m, 0)),
                     pl.BlockSpec((tm, LANES), lambda m: (m, 0)),
                     pl.BlockSpec((SUBLANES, LANES), lambda m: (m, 0))]
        out_shape = [jax.ShapeDtypeStruct((n, d), BF16), jax.ShapeDtypeStruct((n, LANES), F32),
                     jax.ShapeDtypeStruct((n // tm * SUBLANES, LANES), F32)]
    else:
        out_specs = [pl.BlockSpec((tm, d), lambda m: (m, 0))]
        out_shape = [jax.ShapeDtypeStruct((n, d), BF16)]
        if mode == "decay":
            out_specs += [pl.BlockSpec((tm, side), lambda m: (m, 0))]
            out_shape += [jax.ShapeDtypeStruct((n, side), F32)]
    return pl.pallas_call(
        functools.partial(_norm_kernel, mode=mode, n_experts=n_experts),
        grid=(n // tm,),
        in_specs=in_specs,
        out_specs=out_specs,
        out_shape=out_shape,
        compiler_params=_params("arbitrary"),
        name="norm_" + mode,
    )(h, norm_w.reshape(1, d), mod, mod, *extra)


def _mm_kernel(*refs, mode, n_w, w_transposed):
    mm = _dot_nt if w_transposed else _dot
    x_ref = refs[0]
    w_refs = refs[1:1 + n_w]
    wb_refs = refs[len(refs) - n_w:]
    rest = refs[1 + n_w:len(refs) - n_w]

    @pl.when(pl.program_id(1) == 0)
    def _():
        for w_ref, wb_ref in zip(w_refs, wb_refs):
            wb_ref[...] = w_ref[...].astype(BF16)

    x = x_ref[...]
    if mode == "swiglu":
        (o_ref,) = rest
        o_ref[...] = (_silu(mm(x, wb_refs[0][...])) * mm(x, wb_refs[1][...])).astype(o_ref.dtype)
    elif mode == "resid":
        h_ref, g_ref, o_ref = rest
        o_ref[...] = h_ref[...] + g_ref[...] * mm(x, wb_refs[0][...])
    else:
        (o_ref,) = rest
        o_ref[...] = mm(x, wb_refs[0][...]).astype(o_ref.dtype)


def _mm_tiles(m_cap, k, n_cols, n_w, mode):
    tn = _tile(n_cols, 512 if (n_w > 1 or k > 4096) else 1024)
    tm = _tile(m_cap, 512 if (mode == "resid" or k > 4096) else 1024)
    io_bytes = {"plain": 2, "swiglu": 2, "resid": 8}[mode]
    est = n_w * k * tn * (4 + 2) + 2 * tm * k * 2 + 2 * tm * tn * io_bytes + tm * tn * 4 * n_w
    assert est <= VMEM_LIMIT_BYTES, (est, tm, tn)
    return tm, tn


def matmul(x, ws, layer, *, mode, out_dtype, m_cap, n_cols=None, w_transposed=False, h=None, mod=None,
           gate_col=None, row_group=None):
    m_tot, k = x.shape
    n_cols = ws[0].shape[1 if w_transposed else 2] if n_cols is None else n_cols
    tm, tn = _mm_tiles(m_cap, k, n_cols, len(ws), mode)
    if w_transposed:
        w_spec = pl.BlockSpec((None, tn, k), lambda n, m: (layer, n, 0), pipeline_mode=pl.Buffered(1))
    else:
        w_spec = pl.BlockSpec((None, k, tn), lambda n, m: (layer, 0, n), pipeline_mode=pl.Buffered(1))
    in_specs = [pl.BlockSpec((tm, k), lambda n, m: (m, 0))] + [w_spec] * len(ws)
    args = [x, *ws]
    if mode == "resid":
        gate_blk = gate_col // tn
        in_specs += [
            pl.BlockSpec((tm, tn), lambda n, m: (m, n)),
            pl.BlockSpec((None, 1, tn), lambda n, m: (row_group(m, tm), 0, gate_blk + n)),
        ]
        args += [h, mod]
    return pl.pallas_call(
        functools.partial(_mm_kernel, mode=mode, n_w=len(ws), w_transposed=w_transposed),
        grid=(n_cols // tn, m_tot // tm),
        in_specs=in_specs,
        out_specs=pl.BlockSpec((tm, tn), lambda n, m: (m, n)),
        out_shape=jax.ShapeDtypeStruct((m_tot, n_cols), out_dtype),
        scratch_shapes=[pltpu.VMEM((tn, k) if w_transposed else (k, tn), BF16)] * len(ws),
        compiler_params=_params("arbitrary", "arbitrary"),
        name="mm_" + mode,
    )(*args)


SEG_ALIGN = 16


def _run_copies(t, n_experts, lo_ref, g_ref, len_ref, make_copy, wait):
    for e in range(n_experts):
        base = t * n_experts + e

        def body(c, carry, base=base):
            lo = pl.multiple_of(lo_ref[base] + c * SEG_ALIGN, SEG_ALIGN)
            g = pl.multiple_of(g_ref[base] + c * SEG_ALIGN, SEG_ALIGN)
            cp = make_copy(lo, g)
            if wait:
                cp.wait()
            else:
                cp.start()
            return carry

        lax.fori_loop(0, len_ref[base] // SEG_ALIGN, body, 0)


def _dispatch_kernel(lo_ref, g_ref, len_ref, u_ref, meta_ref, *rest, n_experts):
    xs_hbm, sorted_ref, sem = rest[-3:]
    meta = meta_ref[...]
    lane = lax.broadcasted_iota(jnp.int32, meta.shape, 1)
    d1 = jnp.sum(jnp.where(lane == 6, meta, 0.0), axis=-1, keepdims=True).astype(jnp.int32)
    d2 = jnp.sum(jnp.where(lane == 7, meta, 0.0), axis=-1, keepdims=True).astype(jnp.int32)
    slot = lax.broadcasted_iota(jnp.int32, (meta.shape[0], sorted_ref.shape[0]), 1)
    onehot = jnp.logical_or(slot == d1, slot == d2).astype(BF16)
    sorted_ref[...] = _dot_tn(onehot, u_ref[...]).astype(BF16)

    def make_copy(lo, g):
        return pltpu.make_async_copy(sorted_ref.at[pl.ds(lo, SEG_ALIGN)], xs_hbm.at[pl.ds(g, SEG_ALIGN)], sem)

    for wait in (False, True):
        _run_copies(pl.program_id(0), n_experts, lo_ref, g_ref, len_ref, make_copy, wait)


def dispatch_rows(u, meta, seg_lo, seg_g, seg_len, buf, tm, n_experts, r_loc):
    n, d = u.shape
    p_rows = buf.shape[0]
    return pl.pallas_call(
        functools.partial(_dispatch_kernel, n_experts=n_experts),
        grid_spec=pltpu.PrefetchScalarGridSpec(
            num_scalar_prefetch=3,
            grid=(n // tm,),
            in_specs=[
                pl.BlockSpec((tm, d), lambda t, lo, g, ln: (t, 0)),
                pl.BlockSpec((tm, LANES), lambda t, lo, g, ln: (t, 0)),
                pl.BlockSpec(memory_space=pl.ANY),
            ],
            out_specs=pl.BlockSpec(memory_space=pl.ANY),
            scratch_shapes=[pltpu.VMEM((r_loc, d), BF16), pltpu.SemaphoreType.DMA(())],
        ),
        out_shape=jax.ShapeDtypeStruct((p_rows, d), BF16),
        input_output_aliases={5: 0},
        compiler_params=_params("arbitrary"),
        name="moe_dispatch",
    )(seg_lo, seg_g, seg_len, u, meta, buf)


def _expert_mm_kernel(te_ref, na_ref, new_ref, x_ref, *rest, n_w):
    w_refs, o_ref, wb_refs = rest[:n_w], rest[n_w], rest[n_w + 1:]
    i = pl.program_id(1)

    @pl.when(new_ref[i] == 1)
    def _():
        for w_ref, wb_ref in zip(w_refs, wb_refs):
            wb_ref[...] = w_ref[...].astype(BF16)

    @pl.when(i < na_ref[0])
    def _():
        x = x_ref[...]
        if n_w == 2:
            o_ref[...] = (_silu(_dot(x, wb_refs[0][...])) * _dot(x, wb_refs[1][...])).astype(o_ref.dtype)
        else:
            o_ref[...] = _dot(x, wb_refs[0][...]).astype(o_ref.dtype)

    @pl.when(i >= na_ref[0])
    def _():
        o_ref[...] = jnp.zeros_like(o_ref)


def _expert_matmul(x, ws, layer, tn, tile_expert, n_active, new_expert, tm, name):
    p_rows, k = x.shape
    n_cols = ws[0].shape[3]
    w_spec = pl.BlockSpec((None, None, k, tn), lambda j, i, te, na, new: (layer, te[i], 0, j))
    return pl.pallas_call(
        functools.partial(_expert_mm_kernel, n_w=len(ws)),
        grid_spec=pltpu.PrefetchScalarGridSpec(
            num_scalar_prefetch=3,
            grid=(n_cols // tn, p_rows // tm),
            in_specs=[pl.BlockSpec((tm, k), lambda j, i, te, na, new: (i, 0))] + [w_spec] * len(ws),
            out_specs=pl.BlockSpec((tm, tn), lambda j, i, te, na, new: (i, j)),
            scratch_shapes=[pltpu.VMEM((k, tn), BF16)] * len(ws),
        ),
        out_shape=jax.ShapeDtypeStruct((p_rows, n_cols), BF16),
        compiler_params=_params("arbitrary", "arbitrary"),
        name=name,
    )(tile_expert, n_active, new_expert, x, *ws)


def expert_ffn(xs, w1, w3, w2, layer, tile_expert, n_active, new_expert, tm):
    f, d = w2.shape[2:]
    route = (tile_expert, n_active, new_expert, tm)
    hid = _expert_matmul(xs, [w1, w3], layer, _tile(f, 512), *route, "expert_swiglu")
    return _expert_matmul(hid, [w2], layer, _tile(d, 2048), *route, "expert_down")


def _combine_kernel(lo_ref, g_ref, len_ref, ys_hbm, meta_ref, h_ref, gate_ref, o_ref, runs_ref, sem, *, n_experts):
    t = pl.program_id(0)
    buf = t % 2

    def copies(tile, b, wait):
        def make_copy(lo, g):
            return pltpu.make_async_copy(ys_hbm.at[pl.ds(g, SEG_ALIGN)], runs_ref.at[b, pl.ds(lo, SEG_ALIGN)],
                                         sem.at[b])

        _run_copies(tile, n_experts, lo_ref, g_ref, len_ref, make_copy, wait)

    @pl.when(t == 0)
    def _():
        runs_ref[...] = jnp.zeros_like(runs_ref)
        copies(t, buf, False)

    @pl.when(t + 1 < pl.num_programs(0))
    def _():
        copies(t + 1, 1 - buf, False)

    copies(t, buf, True)

    meta = meta_ref[...]
    lane = lax.broadcasted_iota(jnp.int32, meta.shape, 1)

    def field(k):
        return jnp.sum(jnp.where(lane == k, meta, 0.0), axis=-1, keepdims=True)

    slot = lax.broadcasted_iota(jnp.int32, (meta.shape[0], runs_ref.shape[1]), 1)
    runs = runs_ref[buf]
    y1 = _dot((slot == field(6).astype(jnp.int32)).astype(BF16), runs)
    y2 = _dot((slot == field(7).astype(jnp.int32)).astype(BF16), runs)
    o_ref[...] = h_ref[...] + gate_ref[...] * (field(2) * y1 + field(3) * y2)


def combine_experts(ys, meta, seg_lo, seg_g, seg_len, h, mod, gate_col, row_group, tm, n_experts, r_loc):
    n, d = h.shape
    return pl.pallas_call(
        functools.partial(_combine_kernel, n_experts=n_experts),
        grid_spec=pltpu.PrefetchScalarGridSpec(
            num_scalar_prefetch=3,
            grid=(n // tm,),
            in_specs=[
                pl.BlockSpec(memory_space=pl.ANY),
                pl.BlockSpec((tm, LANES), lambda t, lo, g, ln: (t, 0)),
                pl.BlockSpec((tm, d), lambda t, lo, g, ln: (t, 0)),
                pl.BlockSpec((None, 1, d), lambda t, lo, g, ln: (row_group(t, tm), 0, gate_col // d)),
            ],
            out_specs=pl.BlockSpec((tm, d), lambda t, lo, g, ln: (t, 0)),
            scratch_shapes=[pltpu.VMEM((2, r_loc, d), BF16), pltpu.SemaphoreType.DMA((2,))],
        ),
        out_shape=jax.ShapeDtypeStruct((n, d), F32),
        compiler_params=_params("arbitrary"),
        name="moe_combine",
    )(seg_lo, seg_g, seg_len, ys, meta, h, mod)


def moe_ffn(u, meta, counts, h, mod, gate_col, row_group, w1, w3, w2, layer, tm, tm_e, xs_buf):
    n, d = h.shape
    n_e = w1.shape[1]
    n_tt = n // tm
    i32 = jnp.int32
    experts = meta[:, 0:2].astype(i32)
    ranks = meta[:, 4:6].astype(i32)
    cnt = counts.reshape(n_tt, SUBLANES, LANES)[:, 0, :n_e].astype(i32)
    run_len = (cnt + SEG_ALIGN - 1) // SEG_ALIGN * SEG_ALIGN
    run_lo = jnp.cumsum(run_len, axis=1) - run_len
    region = (jnp.sum(run_len, axis=0) + tm_e - 1) // tm_e * tm_e
    region_end = jnp.cumsum(region)
    run_g = (region_end - region)[None, :] + jnp.cumsum(run_len, axis=0) - run_len
    onehot = experts[:, :, None] == jnp.arange(n_e)[None, None, :]
    lo_tok = jnp.repeat(run_lo, tm, axis=0)[:, None, :]
    dest = jnp.sum(jnp.where(onehot, lo_tok, 0), axis=-1) + ranks
    meta = lax.dynamic_update_slice(meta, dest.astype(F32), (0, 6))

    r_loc = 2 * tm + n_e * SEG_ALIGN
    p_rows = -(-(2 * n + n_tt * n_e * SEG_ALIGN + n_e * tm_e) // tm_e) * tm_e
    n_tiles = p_rows // tm_e
    n_active = (region_end[-1] // tm_e).reshape(1).astype(i32)
    tile_row = jnp.arange(n_tiles) * tm_e
    tile_expert = jnp.sum(tile_row[:, None] >= region_end[None, :], axis=1)
    last = jnp.sum(tile_expert * (jnp.arange(n_tiles) == n_active[0] - 1))
    tile_expert = jnp.where(jnp.arange(n_tiles) < n_active[0], tile_expert, last).astype(i32)
    new_expert = jnp.concatenate([jnp.ones((1,), i32), (tile_expert[1:] != tile_expert[:-1]).astype(i32)])

    seg = (run_lo.reshape(-1).astype(i32), run_g.reshape(-1).astype(i32), run_len.reshape(-1).astype(i32))
    buf = jnp.zeros((p_rows, d), BF16) if xs_buf is None else xs_buf
    xs = dispatch_rows(u, meta, *seg, buf, tm, n_e, r_loc)
    ys = expert_ffn(xs, w1, w3, w2, layer, tile_expert, n_active, new_expert, tm_e)
    return combine_experts(ys, meta, *seg, h, mod, gate_col, row_group, tm, n_e, r_loc), xs


def _qkprep_kernel(z_ref, qw_ref, kw_ref, cos_ref, sin_ref, qk_ref, kn_ref, *, n_q, n_kv, hd, scale):
    cos = cos_ref[...]
    sin = sin_ref[...]
    lane = lax.broadcasted_iota(jnp.int32, cos.shape, 1)
    quarter = hd // 4
    first = (lane % (2 * quarter)) < quarter
    for hh in range(n_q + n_kv):
        cols = slice(hh * hd, (hh + 1) * hd)
        x = z_ref[:, cols].astype(F32)
        ms = jnp.mean(x * x, axis=-1, keepdims=True)
        y = x * lax.rsqrt(ms + EPS) * (qw_ref[...] if hh < n_q else kw_ref[...])
        if hh >= n_q:
            kn_ref[:, (hh - n_q) * hd:(hh - n_q + 1) * hd] = y
        partner = jnp.where(first, pltpu.roll(y, hd - quarter, 1), pltpu.roll(y, quarter, 1))
        r = y * cos + partner * sin
        if hh < n_q:
            r = r * scale
        qk_ref[:, cols] = r.astype(BF16)


def qk_prepare(z, q_norm_w, k_norm_w, cos, sin, table_block, tm, *, n_q, n_kv, hd):
    n = z.shape[0]
    w = (n_q + n_kv) * hd
    return pl.pallas_call(
        functools.partial(_qkprep_kernel, n_q=n_q, n_kv=n_kv, hd=hd, scale=hd ** -0.5),
        grid=(n // tm,),
        in_specs=[
            pl.BlockSpec((tm, w), lambda m: (m, 0)),
            pl.BlockSpec((1, hd), lambda m: (0, 0)),
            pl.BlockSpec((1, hd), lambda m: (0, 0)),
            pl.BlockSpec((tm, hd), lambda m: (table_block(m, tm), 0)),
            pl.BlockSpec((tm, hd), lambda m: (table_block(m, tm), 0)),
        ],
        out_specs=[pl.BlockSpec((tm, w), lambda m: (m, 0)), pl.BlockSpec((tm, n_kv * hd), lambda m: (m, 0))],
        out_shape=[jax.ShapeDtypeStruct((n, w), BF16), jax.ShapeDtypeStruct((n, n_kv * hd), F32)],
        compiler_params=_params("arbitrary"),
        name="qk_prep",
    )(z, q_norm_w.reshape(1, hd), k_norm_w.reshape(1, hd), cos, sin)


def _attn_kernel(q_ref, k_ref, v_ref, *rest, group, kvb, hd, has_ctx):
    o_ref = rest[-1]
    for kv in range(kvb):
        kv_cols = slice(kv * hd, (kv + 1) * hd)
        if has_ctx:
            kc_ref, vc_ref = rest[:2]
            kc = kc_ref[:, kv_cols].astype(BF16)
            vc = vc_ref[:, kv_cols].astype(BF16)
        else:
            kc = vc = None
        _attend(q_ref, k_ref[:, kv_cols], v_ref[:, kv_cols], kc, vc, o_ref, kv * group, group, hd)


def _attend(q_ref, k, v, kc, vc, o_ref, head0, group, hd):
    has_ctx = kc is not None
    for gi in range(group):
        cols = slice((head0 + gi) * hd, (head0 + gi + 1) * hd)
        q = q_ref[:, cols]
        s = _dot_nt(q, k)
        mx = jnp.max(s, axis=-1, keepdims=True)
        if has_ctx:
            sc = _dot_nt(q, kc)
            mx = jnp.maximum(mx, jnp.max(sc, axis=-1, keepdims=True))
        p = jnp.exp(s - mx)
        den = jnp.sum(p, axis=-1, keepdims=True)
        o = _dot(p.astype(BF16), v)
        if has_ctx:
            pc = jnp.exp(sc - mx)
            den = den + jnp.sum(pc, axis=-1, keepdims=True)
            o = o + _dot(pc.astype(BF16), vc)
        o_ref[:, cols] = (o * (1.0 / den)).astype(o_ref.dtype)


def _mix_target(mix, mix_shape, args, in_specs):
    if mix is None:
        return jax.ShapeDtypeStruct(mix_shape, BF16), {}
    args.append(mix)
    in_specs.append(pl.BlockSpec(memory_space=pl.ANY))
    return jax.ShapeDtypeStruct(mix.shape, mix.dtype), {len(args) - 1: 0}


def attention(qk, z, row0, n_batch, seq, *, n_q, n_kv, hd, v_col0, mix, mix_shape, ctx=None):
    group = n_q // n_kv
    tq = _tile(seq, 512)
    nq = seq // tq
    rb_q = row0 // tq
    rb_s = row0 // seq
    kvb = n_kv if seq <= 512 else 1
    if (n_q * hd) % (kvb * hd) or v_col0 % (kvb * hd):
        kvb = 1
    qw, kw = kvb * group * hd, kvb * hd
    in_specs = [
        pl.BlockSpec((tq, qw), lambda b, g, i: (rb_q + b * nq + i, g)),
        pl.BlockSpec((seq, kw), lambda b, g, i: (rb_s + b, n_q * hd // kw + g)),
        pl.BlockSpec((seq, kw), lambda b, g, i: (rb_s + b, v_col0 // kw + g)),
    ]
    args = [qk, qk, z]
    if ctx is not None:
        cache_k, cache_v, layer = ctx
        past = cache_k.shape[2]
        c_spec = pl.BlockSpec((None, None, past, kw), lambda b, g, i: (b, layer, 0, g))
        in_specs += [c_spec, c_spec]
        args += [cache_k.reshape(*cache_k.shape[:3], n_kv * hd), cache_v.reshape(*cache_v.shape[:3], n_kv * hd)]
    out_shape, aliases = _mix_target(mix, mix_shape, args, in_specs)
    return pl.pallas_call(
        functools.partial(_attn_kernel, group=group, kvb=kvb, hd=hd, has_ctx=ctx is not None),
        grid=(n_batch, n_kv // kvb, nq),
        in_specs=in_specs,
        out_specs=pl.BlockSpec((tq, qw), lambda b, g, i: (rb_q + b * nq + i, g)),
        out_shape=out_shape,
        input_output_aliases=aliases,
        compiler_params=_params("arbitrary", "arbitrary", "arbitrary"),
        name="attn_ctx" if ctx is not None else "attn",
    )(*args)


def _gla_kernel(q_ref, k_ref, v_ref, g_ref, cf_ref, cb_ref, nw_ref, *rest, seq, chunk, hb, dk, dv, has_s0, want_state,
                scale, n_aliased):
    rest = list(rest)
    s0_ref = rest.pop(0) if has_s0 else None
    del rest[:n_aliased]
    o_ref = rest.pop(0)
    sout_ref = rest.pop(0) if want_state else None
    qdf_ref, qdb_ref, oin_ref, kvf_ref, kvb_ref, spf_ref, spb_ref = rest
    n = seq // chunk
    unroll = True if n * hb <= 8 else max(1, 8 // hb)

    row = lax.broadcasted_iota(jnp.int32, (chunk, chunk), 0)
    col = lax.broadcasted_iota(jnp.int32, (chunk, chunk), 1)
    lower = col <= row
    upper = col >= row

    def rows(i):
        return pl.ds(pl.multiple_of(i * chunk, chunk), chunk)

    def kcols(hh):
        return slice(hh * dk, (hh + 1) * dk)

    def vcols(hh):
        return slice(hh * dv, (hh + 1) * dv)

    def intra(i, hh, cum_ref, mask, tot_row, qd_ref, kv_ref):
        sl = rows(i)
        cum = cum_ref[sl, kcols(hh)]
        tot = cum[tot_row:tot_row + 1, :]
        q = q_ref[sl, kcols(hh)].astype(F32) * scale
        k = k_ref[sl, kcols(hh)].astype(F32)
        v = v_ref[sl, vcols(hh)]
        q_dec = (q * jnp.exp(cum)).astype(BF16)
        k_inv = (k * jnp.exp(-cum)).astype(BF16)
        k_end = (k * jnp.exp(tot - cum)).astype(BF16)
        qd_ref[sl, kcols(hh)] = q_dec
        kv_ref[i * hb + hh] = _dot_tn(v, k_end)
        att = jnp.where(mask, _dot_nt(q_dec, k_inv), 0.0).astype(BF16)
        return _dot(att, v)

    def phase1(i, carry):
        for hh in range(hb):
            oin_ref[rows(i), vcols(hh)] = (intra(i, hh, cf_ref, lower, chunk - 1, qdf_ref, kvf_ref)
                                           + intra(i, hh, cb_ref, upper, 0, qdb_ref, kvb_ref))
        return carry

    lax.fori_loop(0, n, phase1, 0, unroll=unroll)

    def scan(hh, reverse, cum_ref, tot_row, kv_ref, sp_ref, init):
        def step(j, st):
            i = n - 1 - j if reverse else j
            group = pl.multiple_of(i * chunk + tot_row // SUBLANES * SUBLANES, SUBLANES)
            tot = cum_ref[pl.ds(group, SUBLANES), kcols(hh)][tot_row % SUBLANES:tot_row % SUBLANES + 1, :]
            sp_ref[i * hb + hh] = st.astype(BF16)
            return jnp.exp(tot) * st + kv_ref[i * hb + hh]

        return lax.fori_loop(0, n, step, init)

    zero = jnp.zeros(kvf_ref.shape[1:], F32)
    for hh in range(hb):
        st_f = scan(hh, False, cf_ref, chunk - 1, kvf_ref, spf_ref, s0_ref[0, hh].T if has_s0 else zero)
        st_b = scan(hh, True, cb_ref, 0, kvb_ref, spb_ref, s0_ref[1, hh].T if has_s0 else zero)
        if want_state:
            sout_ref[0, hh] = st_f.T
            sout_ref[1, hh] = st_b.T

    def phase3(i, carry):
        sl = rows(i)
        for hh in range(hb):
            o = (oin_ref[sl, vcols(hh)] + _dot_nt(qdf_ref[sl, kcols(hh)], spf_ref[i * hb + hh])
                 + _dot_nt(qdb_ref[sl, kcols(hh)], spb_ref[i * hb + hh]))
            ms = jnp.mean(o * o, axis=-1, keepdims=True)
            y = o * lax.rsqrt(ms + EPS) * nw_ref[...]
            o_ref[sl, vcols(hh)] = (y * _silu(g_ref[sl, vcols(hh)].astype(F32))).astype(o_ref.dtype)
        return carry

    lax.fori_loop(0, n, phase3, 0, unroll=unroll)


def gla(z, la, norm_w, row0, n_batch, seq, mix, mix_col0, *, heads, dk, dv, q_col0, k_col0, v_col0, g_col0, s0=None,
        new_state=None):
    rb = row0 // seq
    n_chunks = seq // GLA_CHUNK
    want_state = new_state is not None
    hb = _tile(heads, max(1, 512 // seq))
    col_starts = ((q_col0, dk), (k_col0, dk), (v_col0, dv), (g_col0, dv), (mix_col0, dv))
    while any(c % (hb * w) for c, w in col_starts):
        hb //= 2
    kw, vw = hb * dk, hb * dv
    in_specs = [
        pl.BlockSpec((seq, kw), lambda b, hh: (rb + b, q_col0 // kw + hh)),
        pl.BlockSpec((seq, kw), lambda b, hh: (rb + b, k_col0 // kw + hh)),
        pl.BlockSpec((seq, vw), lambda b, hh: (rb + b, v_col0 // vw + hh)),
        pl.BlockSpec((seq, vw), lambda b, hh: (rb + b, g_col0 // vw + hh)),
        pl.BlockSpec((seq, kw), lambda b, hh: (rb + b, hh)),
        pl.BlockSpec((seq, kw), lambda b, hh: (rb + b, heads // hb + hh)),
        pl.BlockSpec((1, dv), lambda b, hh: (0, 0)),
    ]
    args = [z, z, z, z, la, la, norm_w.reshape(1, dv)]
    if s0 is not None:
        state, layer = s0
        in_specs += [pl.BlockSpec((None, None, 2, hb, dk, dv), lambda b, hh: (b, layer, 0, hh, 0, 0))]
        args += [state]
    n_fixed = len(args)
    mix_shape, aliases = _mix_target(mix, None, args, in_specs)
    out_specs = [pl.BlockSpec((seq, vw), lambda b, hh: (rb + b, mix_col0 // vw + hh))]
    out_shape = [mix_shape]
    if want_state:
        states, layer_out, depth = new_state
        out_specs += [pl.BlockSpec((None, None, 2, hb, dk, dv), lambda b, hh: (b, layer_out, 0, hh, 0, 0))]
        out_shape += [jax.ShapeDtypeStruct((n_batch, depth, 2, heads, dk, dv), F32)]
        if states is not None:
            args.append(states)
            in_specs.append(pl.BlockSpec(memory_space=pl.ANY))
            aliases[len(args) - 1] = 1
    res = pl.pallas_call(
        functools.partial(_gla_kernel, seq=seq, chunk=GLA_CHUNK, hb=hb, dk=dk, dv=dv, has_s0=s0 is not None,
                          want_state=want_state, scale=dk ** -0.5, n_aliased=len(args) - n_fixed),
        grid=(n_batch, heads // hb),
        in_specs=in_specs,
        out_specs=out_specs,
        out_shape=out_shape,
        input_output_aliases=aliases,
        scratch_shapes=[pltpu.VMEM((seq, kw), BF16), pltpu.VMEM((seq, kw), BF16), pltpu.VMEM((seq, vw), F32),
                        pltpu.VMEM((n_chunks * hb, dv, dk), F32), pltpu.VMEM((n_chunks * hb, dv, dk), F32),
                        pltpu.VMEM((n_chunks * hb, dv, dk), BF16), pltpu.VMEM((n_chunks * hb, dv, dk), BF16)],
        compiler_params=_params("arbitrary", "arbitrary"),
        name="gla_state" if want_state else "gla",
    )(*args)
    return res if want_state else (res[0], None)


def _rope_tables(seq, hd, lead):
    axis_dim = hd // 2
    t = jnp.arange(seq)
    rowp = (t // GRID_W).astype(F32)
    colp = (t % GRID_W).astype(F32)
    inv = ROPE_THETA ** (-jnp.arange(axis_dim // 2, dtype=F32) * 2.0 / axis_dim)
    ang_r = rowp[:, None] * inv
    ang_c = colp[:, None] * inv
    cos = jnp.concatenate([jnp.cos(ang_r), jnp.cos(ang_r), jnp.cos(ang_c), jnp.cos(ang_c)], axis=-1)
    sin = jnp.concatenate([-jnp.sin(ang_r), jnp.sin(ang_r), -jnp.sin(ang_c), jnp.sin(ang_c)], axis=-1)
    cos = jnp.concatenate([jnp.ones((lead, hd), F32), cos], axis=0)
    sin = jnp.concatenate([jnp.zeros((lead, hd), F32), sin], axis=0)
    return cos, sin


def kernel(x_prompt, x_sample, cache_k, cache_v, state_gla, c, c_ctx, norm1_w, norm2_w, w_ada, b_ada, w_in,
           q_norm_w, k_norm_w, gla_up, gla_bias, gla_norm_w, w_out, ffn_w1, ffn_w3, ffn_w2, router_w, moe_w1,
           moe_w3, moe_w2):
    batch, seq1, d = x_prompt.shape
    dec_batch, seq2, _ = x_sample.shape
    depth = w_in.shape[0]
    hd = q_norm_w.shape[-1]
    n_kv = cache_k.shape[3]
    n_q = d // 2 // hd
    heads, dk, dv = state_gla.shape[3:]
    rank = gla_up.shape[2]
    n_experts = router_w.shape[-1]
    attn_w, kv_w, key_w, gla_w = n_q * hd, n_kv * hd, heads * dk, heads * dv
    main_cols = attn_w + 2 * kv_w + 2 * key_w + 2 * gla_w
    q_col0 = attn_w + 2 * kv_w
    k_col0 = q_col0 + key_w
    v_col0 = k_col0 + key_w
    g_col0 = v_col0 + gla_w
    n1, n2 = batch * seq1, dec_batch * seq2
    n = n1 + n2
    assert hd == LANES and n1 % seq2 == 0 and 2 * rank <= LANES and n_experts <= LANES

    def row_group(m, tm):
        tok = m * tm
        return jnp.where(tok < n1, 0, 1 + (tok - n1) // seq2)

    def table_block(m, tm):
        tok = m * tm
        return jnp.where(tok < n1, 0, 1 + ((tok - n1) % seq2) // tm)

    n_rows = -(-(1 + dec_batch) // SUBLANES) * SUBLANES
    cond = jnp.zeros((n_rows, d), F32).at[0].set(c_ctx).at[1:1 + dec_batch].set(c)
    mod_all = ada_modulation(cond, w_ada, b_ada)

    tm_row = _tile(math.gcd(n1, seq2), 256)
    tm_prep = _tile(math.gcd(n1, seq2), 512)
    m_cap = math.gcd(n1, seq2)
    cos, sin = _rope_tables(seq2, hd, tm_prep)

    w_in_t = jnp.swapaxes(w_in, 1, 2)
    h = jnp.concatenate([x_prompt.reshape(n1, d), x_sample.reshape(n2, d)], axis=0)
    new_k, new_v, new_s, xs_buf = [], [], None, None
    for l in range(depth):
        mod = mod_all[l].reshape(n_rows, 1, N_MOD * d)
        w_lr = jnp.zeros((d, LANES), F32).at[:, :2 * rank].set(w_in_t[l, main_cols:, :].T)
        up = jnp.zeros((LANES, 2 * key_w), F32)
        up = up.at[:rank, :key_w].set(gla_up[l, 0]).at[rank:2 * rank, key_w:].set(gla_up[l, 1])
        bias = gla_bias[l].reshape(1, 2 * key_w)

        u, la = norm_modulate(h, norm1_w[l], mod, 1, 0, row_group, tm_row, mode="decay", extra=(w_lr, up, bias))
        z = matmul(u, [w_in_t], l, mode="plain", out_dtype=BF16, m_cap=m_cap, n_cols=main_cols, w_transposed=True)

        qk, kn = qk_prepare(z, q_norm_w[l], k_norm_w[l], cos, sin, table_block, tm_prep, n_q=n_q, n_kv=n_kv, hd=hd)
        new_k.append(kn[:n1].reshape(batch, seq1, n_kv, hd))
        new_v.append(z[:n1, attn_w + kv_w:attn_w + 2 * kv_w].astype(F32).reshape(batch, seq1, n_kv, hd))

        attn_kw = dict(n_q=n_q, n_kv=n_kv, hd=hd, v_col0=attn_w + kv_w, mix_shape=(n, attn_w + gla_w))
        mix = attention(qk, z, 0, batch, seq1, mix=None, **attn_kw)
        mix = attention(qk, z, n1, dec_batch, seq2, mix=mix, ctx=(cache_k, cache_v, l), **attn_kw)
        gla_kw = dict(heads=heads, dk=dk, dv=dv, q_col0=q_col0, k_col0=k_col0, v_col0=v_col0, g_col0=g_col0)
        mix, new_s = gla(z, la, gla_norm_w[l], 0, batch, seq1, mix, attn_w, new_state=(new_s, l, depth), **gla_kw)
        mix, _ = gla(z, la, gla_norm_w[l], n1, dec_batch, seq2, mix, attn_w, s0=(state_gla, l), **gla_kw)

        h = matmul(mix, [w_out], l, mode="resid", out_dtype=F32, m_cap=m_cap, h=h, mod=mod, gate_col=2 * d,
                   row_group=row_group)

        i = l // 2
        if l % 2 == 0:
            (u2,) = norm_modulate(h, norm2_w[l], mod, 4, 3, row_group, tm_row, mode="plain", extra=())
            hid = matmul(u2, [ffn_w1, ffn_w3], i, mode="swiglu", out_dtype=BF16, m_cap=m_cap)
            h = matmul(hid, [ffn_w2], i, mode="resid", out_dtype=F32, m_cap=m_cap, h=h, mod=mod, gate_col=5 * d,
                       row_group=row_group)
        else:
            rw = jnp.zeros((d, LANES), F32).at[:, :n_experts].set(router_w[i])
            xp, meta, counts = norm_modulate(h, norm2_w[l], mod, 4, 3, row_group, tm_row, mode="router",
                                             extra=(rw,), n_experts=n_experts)
            h, xs_buf = moe_ffn(xp, meta, counts, h, mod, 5 * d, row_group, moe_w1, moe_w3, moe_w2, i, tm_row,
                                tm_prep, xs_buf)

    y_prompt = h[:n1].reshape(batch, seq1, d)
    y_sample = h[n1:].reshape(dec_batch, seq2, d)
    return (y_prompt, y_sample, jnp.stack(new_k, axis=1), jnp.stack(new_v, axis=1),
            new_s.astype(x_prompt.dtype))
```

```python
import functools
import math

import jax
import jax.numpy as jnp
from jax import lax
from jax.experimental import pallas as pl
from jax.experimental.pallas import tpu as pltpu

F32 = jnp.float32
BF16 = jnp.bfloat16

EPS = 1e-6
GRID_W = 64
GLA_CHUNK = 64
GLA_TAU = 16.0
ROPE_THETA = 10000.0
N_MOD = 6
LANES = 128
SUBLANES = 8
VMEM_LIMIT_BYTES = 56 * 1024 * 1024


def _params(*semantics):
    return pltpu.CompilerParams(dimension_semantics=semantics, vmem_limit_bytes=VMEM_LIMIT_BYTES)


def _tile(n, cap):
    t = cap
    while t > 1 and n % t:
        t //= 2
    return t


def _dot(a, b):
    return jnp.dot(a, b, preferred_element_type=F32)


def _dot_nt(a, b):
    return lax.dot_general(a, b, (((1,), (1,)), ((), ())), preferred_element_type=F32)


def _dot_tn(a, b):
    return lax.dot_general(a, b, (((0,), (0,)), ((), ())), preferred_element_type=F32)


def _split_bf16(x):
    hi = x.astype(BF16)
    lo = (x - hi.astype(F32)).astype(BF16)
    return hi, lo


def _silu(x):
    return x * jax.nn.sigmoid(x)


def _ada_kernel(c_ref, w_ref, b_ref, o_ref):
    s = _silu(c_ref[...]).astype(BF16)
    o_ref[...] = _dot(s, w_ref[...].astype(BF16)) + b_ref[...]


def ada_modulation(cond, w_ada, b_ada):
    depth, d, n = w_ada.shape
    r = cond.shape[0]
    tn = _tile(n, 1024)
    return pl.pallas_call(
        _ada_kernel,
        grid=(depth, n // tn),
        in_specs=[
            pl.BlockSpec((r, d), lambda l, j: (0, 0)),
            pl.BlockSpec((None, d, tn), lambda l, j: (l, 0, j)),
            pl.BlockSpec((None, 1, tn), lambda l, j: (l, 0, j)),
        ],
        out_specs=pl.BlockSpec((None, r, tn), lambda l, j: (l, 0, j)),
        out_shape=jax.ShapeDtypeStruct((depth, r, n), F32),
        compiler_params=_params("arbitrary", "arbitrary"),
        name="ada",
    )(cond, w_ada, b_ada.reshape(depth, 1, n))


def _pick_group(refs, n_split, tile):
    if n_split is None:
        return refs[0][...], refs[1:]
    return jnp.where(tile < n_split, refs[0][...], refs[1][...]), refs[2:]


def _split_specs(block, n_split, index):
    return [pl.BlockSpec(block, lambda *g: index(jnp.minimum(g[-1], n_split - 1), *g[:-1])),
            pl.BlockSpec(block, lambda *g: index(jnp.maximum(g[-1] - n_split, 0), *g[:-1]))]


def _norm_kernel(*refs, mode, n_experts, n_split):
    x, (nw_ref, sc_ref, sh_ref, *rest) = _pick_group(refs, n_split, pl.program_id(0))
    ms = jnp.mean(x * x, axis=-1, keepdims=True)
    y = x * lax.rsqrt(ms + EPS) * nw_ref[...]
    u = y * (1.0 + sc_ref[...]) + sh_ref[...]
    if mode == "decay":
        wlr_ref, up_ref, bias_ref, u_ref, la_ref = rest
        ub = u.astype(BF16)
        u_ref[...] = ub
        lr = _dot(ub, wlr_ref[...].astype(BF16))
        lr_hi, lr_lo = _split_bf16(lr)
        up_hi, up_lo = _split_bf16(up_ref[...])
        logits = _dot(lr_hi, up_hi) + (_dot(lr_lo, up_hi) + _dot(lr_hi, up_lo)) + bias_ref[...]
        log_sig = jnp.minimum(logits, 0.0) - jnp.log1p(jnp.exp(-jnp.abs(logits)))
        la = log_sig * (1.0 / GLA_TAU)
        tm = la.shape[0]
        half = la.shape[1] // 2
        t_row = lax.broadcasted_iota(jnp.int32, (tm, tm), 0)
        t_col = lax.broadcasted_iota(jnp.int32, (tm, tm), 1)
        same_chunk = (t_row // GLA_CHUNK) == (t_col // GLA_CHUNK)
        tri_f = jnp.logical_and(same_chunk, t_col <= t_row).astype(BF16)
        tri_b = jnp.logical_and(same_chunk, t_col >= t_row).astype(BF16)
        la_hi, la_lo = _split_bf16(la)
        la_ref[:, :half] = _dot(tri_f, la_hi[:, :half]) + _dot(tri_f, la_lo[:, :half])
        la_ref[:, half:] = _dot(tri_b, la_hi[:, half:]) + _dot(tri_b, la_lo[:, half:])
    elif mode == "plain":
        (u_ref,) = rest
        u_ref[...] = u.astype(BF16)
    else:
        rw_ref, u_ref, meta_ref, cnt_ref = rest
        u_ref[...] = u.astype(BF16)
        u_hi, u_lo = _split_bf16(u)
        rw_hi, rw_lo = _split_bf16(rw_ref[...])
        logits = _dot(u_hi, rw_hi) + (_dot(u_lo, rw_hi) + _dot(u_hi, rw_lo))
        lane = lax.broadcasted_iota(jnp.int32, logits.shape, 1)
        neg = jnp.float32(-jnp.inf)
        logits = jnp.where(lane < n_experts, logits, neg)
        v1 = jnp.max(logits, axis=-1, keepdims=True)
        i1 = jnp.min(jnp.where(logits == v1, lane, LANES), axis=-1, keepdims=True)
        rest_l = jnp.where(lane == i1, neg, logits)
        v2 = jnp.max(rest_l, axis=-1, keepdims=True)
        i2 = jnp.min(jnp.where(rest_l == v2, lane, LANES), axis=-1, keepdims=True)
        e2 = jnp.exp(v2 - v1)
        g1 = 1.0 / (1.0 + e2)
        g2 = e2 / (1.0 + e2)

        tm = logits.shape[0]
        chosen = jnp.logical_or(lane == i1, lane == i2)
        t_row = lax.broadcasted_iota(jnp.int32, (tm, tm), 0)
        t_col = lax.broadcasted_iota(jnp.int32, (tm, tm), 1)
        before = _dot((t_col < t_row).astype(BF16), chosen.astype(BF16))
        r1 = jnp.sum(jnp.where(lane == i1, before, 0.0), axis=-1, keepdims=True)
        r2 = jnp.sum(jnp.where(lane == i2, before, 0.0), axis=-1, keepdims=True)
        cnt_ref[...] = jnp.broadcast_to(jnp.sum(chosen.astype(F32), axis=0, keepdims=True), cnt_ref.shape)
        fields = (i1.astype(F32), i2.astype(F32), g1, g2, r1, r2)
        meta = jnp.zeros(logits.shape, F32)
        for pos, val in enumerate(fields):
            meta = jnp.where(lane == pos, val, meta)
        meta_ref[...] = meta


def norm_modulate(h, norm_w, mod, sc_col, sh_col, row_group, tm, *, mode, extra, n_experts=0):
    hs = h if isinstance(h, tuple) else (h,)
    n, d = sum(a.shape[0] for a in hs), hs[0].shape[1]
    n_split = hs[0].shape[0] // tm if len(hs) == 2 else None
    h_specs = (_split_specs((tm, d), n_split, lambda m: (m, 0)) if n_split is not None
               else [pl.BlockSpec((tm, d), lambda m: (m, 0))])
    in_specs = h_specs + [
        pl.BlockSpec((1, d), lambda m: (0, 0)),
        pl.BlockSpec((None, 1, d), lambda m: (row_group(m, tm), 0, sc_col)),
        pl.BlockSpec((None, 1, d), lambda m: (row_group(m, tm), 0, sh_col)),
    ]
    if mode == "decay":
        wlr, up, bias = extra
        in_specs += [
            pl.BlockSpec(wlr.shape, lambda m: (0, 0)),
            pl.BlockSpec(up.shape, lambda m: (0, 0)),
            pl.BlockSpec(bias.shape, lambda m: (0, 0)),
        ]
        side = up.shape[1]
    elif mode == "router":
        (rw,) = extra
        in_specs += [pl.BlockSpec(rw.shape, lambda m: (0, 0))]
    if mode == "router":
        out_specs = [pl.BlockSpec((tm, d), lambda m: (m, 0)),
                     pl.BlockSpec((tm, LANES), lambda m: (m, 0)),
                     pl.BlockSpec((SUBLANES, LANES), lambda m: (m, 0))]
        out_shape = [jax.ShapeDtypeStruct((n, d), BF16), jax.ShapeDtypeStruct((n, LANES), F32),
                     jax.ShapeDtypeStruct((n // tm * SUBLANES, LANES), F32)]
    else:
        out_specs = [pl.BlockSpec((tm, d), lambda m: (m, 0))]
        out_shape = [jax.ShapeDtypeStruct((n, d), BF16)]
        if mode == "decay":
            out_specs += [pl.BlockSpec((tm, side), lambda m: (m, 0))]
            out_shape += [jax.ShapeDtypeStruct((n, side), F32)]
    return pl.pallas_call(
        functools.partial(_norm_kernel, mode=mode, n_experts=n_experts, n_split=n_split),
        grid=(n // tm,),
        in_specs=in_specs,
        out_specs=out_specs,
        out_shape=out_shape,
        compiler_params=_params("arbitrary"),
        name="norm_" + mode,
    )(*hs, norm_w.reshape(1, d), mod, mod, *extra)


def _mm_kernel(*refs, mode, n_w, w_transposed, n_split):
    mm = _dot_nt if w_transposed else _dot
    x_ref = refs[0]
    w_refs = refs[1:1 + n_w]
    wb_refs = refs[len(refs) - n_w:]
    rest = refs[1 + n_w:len(refs) - n_w]

    @pl.when(pl.program_id(1) == 0)
    def _():
        for w_ref, wb_ref in zip(w_refs, wb_refs):
            wb_ref[...] = w_ref[...].astype(BF16)

    x = x_ref[...]
    if mode == "swiglu":
        (o_ref,) = rest
        o_ref[...] = (_silu(mm(x, wb_refs[0][...])) * mm(x, wb_refs[1][...])).astype(o_ref.dtype)
    elif mode == "resid":
        h_tile, (g_ref, o_ref) = _pick_group(rest, n_split, pl.program_id(1))
        o_ref[...] = h_tile + g_ref[...] * mm(x, wb_refs[0][...])
    else:
        (o_ref,) = rest
        o_ref[...] = mm(x, wb_refs[0][...]).astype(o_ref.dtype)


def _mm_tiles(m_cap, k, n_cols, n_w, mode):
    tn = _tile(n_cols, 512 if (n_w > 1 or k > 4096) else 1024)
    tm = _tile(m_cap, 512 if (mode == "resid" or k > 4096) else 1024)
    io_bytes = {"plain": 2, "swiglu": 2, "resid": 8}[mode]
    est = n_w * k * tn * (4 + 2) + 2 * tm * k * 2 + 2 * tm * tn * io_bytes + tm * tn * 4 * n_w
    assert est <= VMEM_LIMIT_BYTES, (est, tm, tn)
    return tm, tn


def matmul(x, ws, layer, *, mode, out_dtype, m_cap, n_cols=None, w_transposed=False, h=None, mod=None,
           gate_col=None, row_group=None):
    m_tot, k = x.shape
    n_cols = ws[0].shape[1 if w_transposed else 2] if n_cols is None else n_cols
    tm, tn = _mm_tiles(m_cap, k, n_cols, len(ws), mode)
    if w_transposed:
        w_spec = pl.BlockSpec((None, tn, k), lambda n, m: (layer, n, 0), pipeline_mode=pl.Buffered(1))
    else:
        w_spec = pl.BlockSpec((None, k, tn), lambda n, m: (layer, 0, n), pipeline_mode=pl.Buffered(1))
    in_specs = [pl.BlockSpec((tm, k), lambda n, m: (m, 0))] + [w_spec] * len(ws)
    args = [x, *ws]
    n_split = None
    if mode == "resid":
        gate_blk = gate_col // tn
        hs = h if isinstance(h, tuple) else (h,)
        if len(hs) == 2:
            n_split = hs[0].shape[0] // tm
            in_specs += _split_specs((tm, tn), n_split, lambda m, n: (m, n))
        else:
            in_specs += [pl.BlockSpec((tm, tn), lambda n, m: (m, n))]
        in_specs += [pl.BlockSpec((None, 1, tn), lambda n, m: (row_group(m, tm), 0, gate_blk + n))]
        args += [*hs, mod]
    return pl.pallas_call(
        functools.partial(_mm_kernel, mode=mode, n_w=len(ws), w_transposed=w_transposed, n_split=n_split),
        grid=(n_cols // tn, m_tot // tm),
        in_specs=in_specs,
        out_specs=pl.BlockSpec((tm, tn), lambda n, m: (m, n)),
        out_shape=jax.ShapeDtypeStruct((m_tot, n_cols), out_dtype),
        scratch_shapes=[pltpu.VMEM((tn, k) if w_transposed else (k, tn), BF16)] * len(ws),
        compiler_params=_params("arbitrary", "arbitrary"),
        name="mm_" + mode,
    )(*args)


SEG_ALIGN = 16


def _run_copies(t, n_experts, lo_ref, g_ref, len_ref, make_copy, wait):
    for e in range(n_experts):
        base = t * n_experts + e

        def body(c, carry, base=base):
            lo = pl.multiple_of(lo_ref[base] + c * SEG_ALIGN, SEG_ALIGN)
            g = pl.multiple_of(g_ref[base] + c * SEG_ALIGN, SEG_ALIGN)
            cp = make_copy(lo, g)
            if wait:
                cp.wait()
            else:
                cp.start()
            return carry

        lax.fori_loop(0, len_ref[base] // SEG_ALIGN, body, 0)


def _dispatch_kernel(lo_ref, g_ref, len_ref, u_ref, meta_ref, *rest, n_experts):
    xs_hbm, sorted_ref, sem = rest[-3:]
    meta = meta_ref[...]
    lane = lax.broadcasted_iota(jnp.int32, meta.shape, 1)
    d1 = jnp.sum(jnp.where(lane == 6, meta, 0.0), axis=-1, keepdims=True).astype(jnp.int32)
    d2 = jnp.sum(jnp.where(lane == 7, meta, 0.0), axis=-1, keepdims=True).astype(jnp.int32)
    slot = lax.broadcasted_iota(jnp.int32, (meta.shape[0], sorted_ref.shape[0]), 1)
    onehot = jnp.logical_or(slot == d1, slot == d2).astype(BF16)
    sorted_ref[...] = _dot_tn(onehot, u_ref[...]).astype(BF16)

    def make_copy(lo, g):
        return pltpu.make_async_copy(sorted_ref.at[pl.ds(lo, SEG_ALIGN)], xs_hbm.at[pl.ds(g, SEG_ALIGN)], sem)

    for wait in (False, True):
        _run_copies(pl.program_id(0), n_experts, lo_ref, g_ref, len_ref, make_copy, wait)


def dispatch_rows(u, meta, seg_lo, seg_g, seg_len, buf, tm, n_experts, r_loc):
    n, d = u.shape
    p_rows = buf.shape[0]
    return pl.pallas_call(
        functools.partial(_dispatch_kernel, n_experts=n_experts),
        grid_spec=pltpu.PrefetchScalarGridSpec(
            num_scalar_prefetch=3,
            grid=(n // tm,),
            in_specs=[
                pl.BlockSpec((tm, d), lambda t, lo, g, ln: (t, 0)),
                pl.BlockSpec((tm, LANES), lambda t, lo, g, ln: (t, 0)),
                pl.BlockSpec(memory_space=pl.ANY),
            ],
            out_specs=pl.BlockSpec(memory_space=pl.ANY),
            scratch_shapes=[pltpu.VMEM((r_loc, d), BF16), pltpu.SemaphoreType.DMA(())],
        ),
        out_shape=jax.ShapeDtypeStruct((p_rows, d), BF16),
        input_output_aliases={5: 0},
        compiler_params=_params("arbitrary"),
        name="moe_dispatch",
    )(seg_lo, seg_g, seg_len, u, meta, buf)


def _expert_mm_kernel(te_ref, na_ref, new_ref, x_ref, *rest, n_w):
    w_refs, o_ref, wb_refs = rest[:n_w], rest[n_w], rest[n_w + 1:]
    i = pl.program_id(1)

    @pl.when(new_ref[i] == 1)
    def _():
        for w_ref, wb_ref in zip(w_refs, wb_refs):
            wb_ref[...] = w_ref[...].astype(BF16)

    @pl.when(i < na_ref[0])
    def _():
        x = x_ref[...]
        if n_w == 2:
            o_ref[...] = (_silu(_dot(x, wb_refs[0][...])) * _dot(x, wb_refs[1][...])).astype(o_ref.dtype)
        else:
            o_ref[...] = _dot(x, wb_refs[0][...]).astype(o_ref.dtype)

    @pl.when(i >= na_ref[0])
    def _():
        o_ref[...] = jnp.zeros_like(o_ref)


def _expert_matmul(x, ws, layer, tn, tile_expert, n_active, new_expert, tm, name):
    p_rows, k = x.shape
    n_cols = ws[0].shape[3]
    w_spec = pl.BlockSpec((None, None, k, tn), lambda j, i, te, na, new: (layer, te[i], 0, j))
    return pl.pallas_call(
        functools.partial(_expert_mm_kernel, n_w=len(ws)),
        grid_spec=pltpu.PrefetchScalarGridSpec(
            num_scalar_prefetch=3,
            grid=(n_cols // tn, p_rows // tm),
            in_specs=[pl.BlockSpec((tm, k), lambda j, i, te, na, new: (i, 0))] + [w_spec] * len(ws),
            out_specs=pl.BlockSpec((tm, tn), lambda j, i, te, na, new: (i, j)),
            scratch_shapes=[pltpu.VMEM((k, tn), BF16)] * len(ws),
        ),
        out_shape=jax.ShapeDtypeStruct((p_rows, n_cols), BF16),
        compiler_params=_params("arbitrary", "arbitrary"),
        name=name,
    )(tile_expert, n_active, new_expert, x, *ws)


def expert_ffn(xs, w1, w3, w2, layer, tile_expert, n_active, new_expert, tm):
    f, d = w2.shape[2:]
    route = (tile_expert, n_active, new_expert, tm)
    hid = _expert_matmul(xs, [w1, w3], layer, _tile(f, 512), *route, "expert_swiglu")
    return _expert_matmul(hid, [w2], layer, _tile(d, 2048), *route, "expert_down")


def _combine_kernel(lo_ref, g_ref, len_ref, ys_hbm, meta_ref, h_ref, gate_ref, o_ref, runs_ref, sem, *, n_experts):
    t = pl.program_id(0)
    buf = t % 2

    def copies(tile, b, wait):
        def make_copy(lo, g):
            return pltpu.make_async_copy(ys_hbm.at[pl.ds(g, SEG_ALIGN)], runs_ref.at[b, pl.ds(lo, SEG_ALIGN)],
                                         sem.at[b])

        _run_copies(tile, n_experts, lo_ref, g_ref, len_ref, make_copy, wait)

    @pl.when(t == 0)
    def _():
        runs_ref[...] = jnp.zeros_like(runs_ref)
        copies(t, buf, False)

    @pl.when(t + 1 < pl.num_programs(0))
    def _():
        copies(t + 1, 1 - buf, False)

    copies(t, buf, True)

    meta = meta_ref[...]
    lane = lax.broadcasted_iota(jnp.int32, meta.shape, 1)

    def field(k):
        return jnp.sum(jnp.where(lane == k, meta, 0.0), axis=-1, keepdims=True)

    slot = lax.broadcasted_iota(jnp.int32, (meta.shape[0], runs_ref.shape[1]), 1)
    runs = runs_ref[buf]
    y1 = _dot((slot == field(6).astype(jnp.int32)).astype(BF16), runs)
    y2 = _dot((slot == field(7).astype(jnp.int32)).astype(BF16), runs)
    o_ref[...] = h_ref[...] + gate_ref[...] * (field(2) * y1 + field(3) * y2)


def combine_experts(ys, meta, seg_lo, seg_g, seg_len, h, mod, gate_col, row_group, tm, n_experts, r_loc):
    n, d = h.shape
    return pl.pallas_call(
        functools.partial(_combine_kernel, n_experts=n_experts),
        grid_spec=pltpu.PrefetchScalarGridSpec(
            num_scalar_prefetch=3,
            grid=(n // tm,),
            in_specs=[
                pl.BlockSpec(memory_space=pl.ANY),
                pl.BlockSpec((tm, LANES), lambda t, lo, g, ln: (t, 0)),
                pl.BlockSpec((tm, d), lambda t, lo, g, ln: (t, 0)),
                pl.BlockSpec((None, 1, d), lambda t, lo, g, ln: (row_group(t, tm), 0, gate_col // d)),
            ],
            out_specs=pl.BlockSpec((tm, d), lambda t, lo, g, ln: (t, 0)),
            scratch_shapes=[pltpu.VMEM((2, r_loc, d), BF16), pltpu.SemaphoreType.DMA((2,))],
        ),
        out_shape=jax.ShapeDtypeStruct((n, d), F32),
        compiler_params=_params("arbitrary"),
        name="moe_combine",
    )(seg_lo, seg_g, seg_len, ys, meta, h, mod)


def moe_ffn(u, meta, counts, h, mod, gate_col, row_group, w1, w3, w2, layer, tm, tm_e, xs_buf):
    n, d = h.shape
    n_e = w1.shape[1]
    n_tt = n // tm
    i32 = jnp.int32
    experts = meta[:, 0:2].astype(i32)
    ranks = meta[:, 4:6].astype(i32)
    cnt = counts.reshape(n_tt, SUBLANES, LANES)[:, 0, :n_e].astype(i32)
    run_len = (cnt + SEG_ALIGN - 1) // SEG_ALIGN * SEG_ALIGN
    run_lo = jnp.cumsum(run_len, axis=1) - run_len
    region = (jnp.sum(run_len, axis=0) + tm_e - 1) // tm_e * tm_e
    region_end = jnp.cumsum(region)
    run_g = (region_end - region)[None, :] + jnp.cumsum(run_len, axis=0) - run_len
    onehot = experts[:, :, None] == jnp.arange(n_e)[None, None, :]
    lo_tok = jnp.repeat(run_lo, tm, axis=0)[:, None, :]
    dest = jnp.sum(jnp.where(onehot, lo_tok, 0), axis=-1) + ranks
    meta = lax.dynamic_update_slice(meta, dest.astype(F32), (0, 6))

    r_loc = 2 * tm + n_e * SEG_ALIGN
    p_rows = -(-(2 * n + n_tt * n_e * SEG_ALIGN + n_e * tm_e) // tm_e) * tm_e
    n_tiles = p_rows // tm_e
    n_active = (region_end[-1] // tm_e).reshape(1).astype(i32)
    tile_row = jnp.arange(n_tiles) * tm_e
    tile_expert = jnp.sum(tile_row[:, None] >= region_end[None, :], axis=1)
    last = jnp.sum(tile_expert * (jnp.arange(n_tiles) == n_active[0] - 1))
    tile_expert = jnp.where(jnp.arange(n_tiles) < n_active[0], tile_expert, last).astype(i32)
    new_expert = jnp.concatenate([jnp.ones((1,), i32), (tile_expert[1:] != tile_expert[:-1]).astype(i32)])

    seg = (run_lo.reshape(-1).astype(i32), run_g.reshape(-1).astype(i32), run_len.reshape(-1).astype(i32))
    buf = jnp.zeros((p_rows, d), BF16) if xs_buf is None else xs_buf
    xs = dispatch_rows(u, meta, *seg, buf, tm, n_e, r_loc)
    ys = expert_ffn(xs, w1, w3, w2, layer, tile_expert, n_active, new_expert, tm_e)
    return combine_experts(ys, meta, *seg, h, mod, gate_col, row_group, tm, n_e, r_loc), xs


def _qkprep_kernel(z_ref, qw_ref, kw_ref, cos_ref, sin_ref, qk_ref, kn_ref, *, n_q, n_kv, hd, scale):
    cos = cos_ref[...]
    sin = sin_ref[...]
    lane = lax.broadcasted_iota(jnp.int32, cos.shape, 1)
    quarter = hd // 4
    first = (lane % (2 * quarter)) < quarter
    for hh in range(n_q + n_kv):
        cols = slice(hh * hd, (hh + 1) * hd)
        x = z_ref[:, cols].astype(F32)
        ms = jnp.mean(x * x, axis=-1, keepdims=True)
        y = x * lax.rsqrt(ms + EPS) * (qw_ref[...] if hh < n_q else kw_ref[...])
        if hh >= n_q:
            kn_ref[:, (hh - n_q) * hd:(hh - n_q + 1) * hd] = y
        partner = jnp.where(first, pltpu.roll(y, hd - quarter, 1), pltpu.roll(y, quarter, 1))
        r = y * cos + partner * sin
        if hh < n_q:
            r = r * scale
        qk_ref[:, cols] = r.astype(BF16)


def qk_prepare(z, q_norm_w, k_norm_w, cos, sin, table_block, tm, *, n_q, n_kv, hd):
    n = z.shape[0]
    w = (n_q + n_kv) * hd
    return pl.pallas_call(
        functools.partial(_qkprep_kernel, n_q=n_q, n_kv=n_kv, hd=hd, scale=hd ** -0.5),
        grid=(n // tm,),
        in_specs=[
            pl.BlockSpec((tm, w), lambda m: (m, 0)),
            pl.BlockSpec((1, hd), lambda m: (0, 0)),
            pl.BlockSpec((1, hd), lambda m: (0, 0)),
            pl.BlockSpec((tm, hd), lambda m: (table_block(m, tm), 0)),
            pl.BlockSpec((tm, hd), lambda m: (table_block(m, tm), 0)),
        ],
        out_specs=[pl.BlockSpec((tm, w), lambda m: (m, 0)), pl.BlockSpec((tm, n_kv * hd), lambda m: (m, 0))],
        out_shape=[jax.ShapeDtypeStruct((n, w), BF16), jax.ShapeDtypeStruct((n, n_kv * hd), F32)],
        compiler_params=_params("arbitrary"),
        name="qk_prep",
    )(z, q_norm_w.reshape(1, hd), k_norm_w.reshape(1, hd), cos, sin)


def _attn_kernel(q_ref, k_ref, v_ref, *rest, group, kvb, hd, has_ctx):
    o_ref = rest[-1]
    for kv in range(kvb):
        kv_cols = slice(kv * hd, (kv + 1) * hd)
        if has_ctx:
            kc_ref, vc_ref = rest[:2]
            kc = kc_ref[:, kv_cols].astype(BF16)
            vc = vc_ref[:, kv_cols].astype(BF16)
        else:
            kc = vc = None
        _attend(q_ref, k_ref[:, kv_cols], v_ref[:, kv_cols], kc, vc, o_ref, kv * group, group, hd)


def _attend(q_ref, k, v, kc, vc, o_ref, head0, group, hd):
    has_ctx = kc is not None
    for gi in range(group):
        cols = slice((head0 + gi) * hd, (head0 + gi + 1) * hd)
        q = q_ref[:, cols]
        s = _dot_nt(q, k)
        mx = jnp.max(s, axis=-1, keepdims=True)
        if has_ctx:
            sc = _dot_nt(q, kc)
            mx = jnp.maximum(mx, jnp.max(sc, axis=-1, keepdims=True))
        p = jnp.exp(s - mx)
        den = jnp.sum(p, axis=-1, keepdims=True)
        o = _dot(p.astype(BF16), v)
        if has_ctx:
            pc = jnp.exp(sc - mx)
            den = den + jnp.sum(pc, axis=-1, keepdims=True)
            o = o + _dot(pc.astype(BF16), vc)
        o_ref[:, cols] = (o * (1.0 / den)).astype(o_ref.dtype)


def _mix_target(mix, mix_shape, args, in_specs):
    if mix is None:
        return jax.ShapeDtypeStruct(mix_shape, BF16), {}
    args.append(mix)
    in_specs.append(pl.BlockSpec(memory_space=pl.ANY))
    return jax.ShapeDtypeStruct(mix.shape, mix.dtype), {len(args) - 1: 0}


def attention(qk, z, row0, n_batch, seq, *, n_q, n_kv, hd, v_col0, mix, mix_shape, ctx=None):
    group = n_q // n_kv
    tq = _tile(seq, 512)
    nq = seq // tq
    rb_q = row0 // tq
    rb_s = row0 // seq
    kvb = n_kv if seq <= 512 else 1
    if (n_q * hd) % (kvb * hd) or v_col0 % (kvb * hd):
        kvb = 1
    qw, kw = kvb * group * hd, kvb * hd
    in_specs = [
        pl.BlockSpec((tq, qw), lambda b, g, i: (rb_q + b * nq + i, g)),
        pl.BlockSpec((seq, kw), lambda b, g, i: (rb_s + b, n_q * hd // kw + g)),
        pl.BlockSpec((seq, kw), lambda b, g, i: (rb_s + b, v_col0 // kw + g)),
    ]
    args = [qk, qk, z]
    if ctx is not None:
        cache_k, cache_v, layer = ctx
        past = cache_k.shape[2]
        c_spec = pl.BlockSpec((None, None, past, kw), lambda b, g, i: (b, layer, 0, g))
        in_specs += [c_spec, c_spec]
        args += [cache_k.reshape(*cache_k.shape[:3], n_kv * hd), cache_v.reshape(*cache_v.shape[:3], n_kv * hd)]
    out_shape, aliases = _mix_target(mix, mix_shape, args, in_specs)
    return pl.pallas_call(
        functools.partial(_attn_kernel, group=group, kvb=kvb, hd=hd, has_ctx=ctx is not None),
        grid=(n_batch, n_kv // kvb, nq),
        in_specs=in_specs,
        out_specs=pl.BlockSpec((tq, qw), lambda b, g, i: (rb_q + b * nq + i, g)),
        out_shape=out_shape,
        input_output_aliases=aliases,
        compiler_params=_params("arbitrary", "arbitrary", "arbitrary"),
        name="attn_ctx" if ctx is not None else "attn",
    )(*args)


def _gla_kernel(q_ref, k_ref, v_ref, g_ref, cf_ref, cb_ref, nw_ref, *rest, seq, chunk, hb, dk, dv, has_s0, want_state,
                scale, n_aliased):
    rest = list(rest)
    s0_ref = rest.pop(0) if has_s0 else None
    del rest[:n_aliased]
    o_ref = rest.pop(0)
    sout_ref = rest.pop(0) if want_state else None
    qdf_ref, qdb_ref, oin_ref, kvf_ref, kvb_ref, spf_ref, spb_ref = rest
    n = seq // chunk
    unroll = True if n * hb <= 8 else max(1, 8 // hb)

    row = lax.broadcasted_iota(jnp.int32, (chunk, chunk), 0)
    col = lax.broadcasted_iota(jnp.int32, (chunk, chunk), 1)
    lower = col <= row
    upper = col >= row

    def rows(i):
        return pl.ds(pl.multiple_of(i * chunk, chunk), chunk)

    def kcols(hh):
        return slice(hh * dk, (hh + 1) * dk)

    def vcols(hh):
        return slice(hh * dv, (hh + 1) * dv)

    def intra(i, hh, cum_ref, mask, tot_row, qd_ref, kv_ref):
        sl = rows(i)
        cum = cum_ref[sl, kcols(hh)]
        tot = cum[tot_row:tot_row + 1, :]
        q = q_ref[sl, kcols(hh)].astype(F32) * scale
        k = k_ref[sl, kcols(hh)].astype(F32)
        v = v_ref[sl, vcols(hh)]
        q_dec = (q * jnp.exp(cum)).astype(BF16)
        k_inv = (k * jnp.exp(-cum)).astype(BF16)
        k_end = (k * jnp.exp(tot - cum)).astype(BF16)
        qd_ref[sl, kcols(hh)] = q_dec
        kv_ref[i * hb + hh] = _dot_tn(v, k_end)
        att = jnp.where(mask, _dot_nt(q_dec, k_inv), 0.0).astype(BF16)
        return _dot(att, v)

    def phase1(i, carry):
        for hh in range(hb):
            oin_ref[rows(i), vcols(hh)] = (intra(i, hh, cf_ref, lower, chunk - 1, qdf_ref, kvf_ref)
                                           + intra(i, hh, cb_ref, upper, 0, qdb_ref, kvb_ref))
        return carry

    lax.fori_loop(0, n, phase1, 0, unroll=unroll)

    def scan(hh, reverse, cum_ref, tot_row, kv_ref, sp_ref, init):
        def step(j, st):
            i = n - 1 - j if reverse else j
            group = pl.multiple_of(i * chunk + tot_row // SUBLANES * SUBLANES, SUBLANES)
            tot = cum_ref[pl.ds(group, SUBLANES), kcols(hh)][tot_row % SUBLANES:tot_row % SUBLANES + 1, :]
            sp_ref[i * hb + hh] = st.astype(BF16)
            return jnp.exp(tot) * st + kv_ref[i * hb + hh]

        return lax.fori_loop(0, n, step, init)

    zero = jnp.zeros(kvf_ref.shape[1:], F32)
    for hh in range(hb):
        st_f = scan(hh, False, cf_ref, chunk - 1, kvf_ref, spf_ref, s0_ref[0, hh].T if has_s0 else zero)
        st_b = scan(hh, True, cb_ref, 0, kvb_ref, spb_ref, s0_ref[1, hh].T if has_s0 else zero)
        if want_state:
            sout_ref[0, hh] = st_f.T
            sout_ref[1, hh] = st_b.T

    def phase3(i, carry):
        sl = rows(i)
        for hh in range(hb):
            o = (oin_ref[sl, vcols(hh)] + _dot_nt(qdf_ref[sl, kcols(hh)], spf_ref[i * hb + hh])
                 + _dot_nt(qdb_ref[sl, kcols(hh)], spb_ref[i * hb + hh]))
            ms = jnp.mean(o * o, axis=-1, keepdims=True)
            y = o * lax.rsqrt(ms + EPS) * nw_ref[...]
            o_ref[sl, vcols(hh)] = (y * _silu(g_ref[sl, vcols(hh)].astype(F32))).astype(o_ref.dtype)
        return carry

    lax.fori_loop(0, n, phase3, 0, unroll=unroll)


def gla(z, la, norm_w, row0, n_batch, seq, mix, mix_col0, *, heads, dk, dv, q_col0, k_col0, v_col0, g_col0, s0=None,
        new_state=None):
    rb = row0 // seq
    n_chunks = seq // GLA_CHUNK
    want_state = new_state is not None
    hb = _tile(heads, max(1, 512 // seq))
    col_starts = ((q_col0, dk), (k_col0, dk), (v_col0, dv), (g_col0, dv), (mix_col0, dv))
    while any(c % (hb * w) for c, w in col_starts):
        hb //= 2
    kw, vw = hb * dk, hb * dv
    in_specs = [
        pl.BlockSpec((seq, kw), lambda b, hh: (rb + b, q_col0 // kw + hh)),
        pl.BlockSpec((seq, kw), lambda b, hh: (rb + b, k_col0 // kw + hh)),
        pl.BlockSpec((seq, vw), lambda b, hh: (rb + b, v_col0 // vw + hh)),
        pl.BlockSpec((seq, vw), lambda b, hh: (rb + b, g_col0 // vw + hh)),
        pl.BlockSpec((seq, kw), lambda b, hh: (rb + b, hh)),
        pl.BlockSpec((seq, kw), lambda b, hh: (rb + b, heads // hb + hh)),
        pl.BlockSpec((1, dv), lambda b, hh: (0, 0)),
    ]
    args = [z, z, z, z, la, la, norm_w.reshape(1, dv)]
    if s0 is not None:
        state, layer = s0
        in_specs += [pl.BlockSpec((None, None, 2, hb, dk, dv), lambda b, hh: (b, layer, 0, hh, 0, 0))]
        args += [state]
    n_fixed = len(args)
    mix_shape, aliases = _mix_target(mix, None, args, in_specs)
    out_specs = [pl.BlockSpec((seq, vw), lambda b, hh: (rb + b, mix_col0 // vw + hh))]
    out_shape = [mix_shape]
    if want_state:
        states, layer_out, depth = new_state
        out_specs += [pl.BlockSpec((None, None, 2, hb, dk, dv), lambda b, hh: (b, layer_out, 0, hh, 0, 0))]
        out_shape += [jax.ShapeDtypeStruct((n_batch, depth, 2, heads, dk, dv), F32)]
        if states is not None:
            args.append(states)
            in_specs.append(pl.BlockSpec(memory_space=pl.ANY))
            aliases[len(args) - 1] = 1
    res = pl.pallas_call(
        functools.partial(_gla_kernel, seq=seq, chunk=GLA_CHUNK, hb=hb, dk=dk, dv=dv, has_s0=s0 is not None,
                          want_state=want_state, scale=dk ** -0.5, n_aliased=len(args) - n_fixed),
        grid=(n_batch, heads // hb),
        in_specs=in_specs,
        out_specs=out_specs,
        out_shape=out_shape,
        input_output_aliases=aliases,
        scratch_shapes=[pltpu.VMEM((seq, kw), BF16), pltpu.VMEM((seq, kw), BF16), pltpu.VMEM((seq, vw), F32),
                        pltpu.VMEM((n_chunks * hb, dv, dk), F32), pltpu.VMEM((n_chunks * hb, dv, dk), F32),
                        pltpu.VMEM((n_chunks * hb, dv, dk), BF16), pltpu.VMEM((n_chunks * hb, dv, dk), BF16)],
        compiler_params=_params("arbitrary", "arbitrary"),
        name="gla_state" if want_state else "gla",
    )(*args)
    return res if want_state else (res[0], None)


def _rope_tables(seq, hd, lead):
    axis_dim = hd // 2
    t = jnp.arange(seq)
    rowp = (t // GRID_W).astype(F32)
    colp = (t % GRID_W).astype(F32)
    inv = ROPE_THETA ** (-jnp.arange(axis_dim // 2, dtype=F32) * 2.0 / axis_dim)
    ang_r = rowp[:, None] * inv
    ang_c = colp[:, None] * inv
    cos = jnp.concatenate([jnp.cos(ang_r), jnp.cos(ang_r), jnp.cos(ang_c), jnp.cos(ang_c)], axis=-1)
    sin = jnp.concatenate([-jnp.sin(ang_r), jnp.sin(ang_r), -jnp.sin(ang_c), jnp.sin(ang_c)], axis=-1)
    cos = jnp.concatenate([jnp.ones((lead, hd), F32), cos], axis=0)
    sin = jnp.concatenate([jnp.zeros((lead, hd), F32), sin], axis=0)
    return cos, sin


def kernel(x_prompt, x_sample, cache_k, cache_v, state_gla, c, c_ctx, norm1_w, norm2_w, w_ada, b_ada, w_in,
           q_norm_w, k_norm_w, gla_up, gla_bias, gla_norm_w, w_out, ffn_w1, ffn_w3, ffn_w2, router_w, moe_w1,
           moe_w3, moe_w2):
    batch, seq1, d = x_prompt.shape
    dec_batch, seq2, _ = x_sample.shape
    depth = w_in.shape[0]
    hd = q_norm_w.shape[-1]
    n_kv = cache_k.shape[3]
    n_q = d // 2 // hd
    heads, dk, dv = state_gla.shape[3:]
    rank = gla_up.shape[2]
    n_experts = router_w.shape[-1]
    attn_w, kv_w, key_w, gla_w = n_q * hd, n_kv * hd, heads * dk, heads * dv
    main_cols = attn_w + 2 * kv_w + 2 * key_w + 2 * gla_w
    q_col0 = attn_w + 2 * kv_w
    k_col0 = q_col0 + key_w
    v_col0 = k_col0 + key_w
    g_col0 = v_col0 + gla_w
    n1, n2 = batch * seq1, dec_batch * seq2
    n = n1 + n2
    assert hd == LANES and n1 % seq2 == 0 and 2 * rank <= LANES and n_experts <= LANES

    def row_group(m, tm):
        tok = m * tm
        return jnp.where(tok < n1, 0, 1 + (tok - n1) // seq2)

    def table_block(m, tm):
        tok = m * tm
        return jnp.where(tok < n1, 0, 1 + ((tok - n1) % seq2) // tm)

    n_rows = -(-(1 + dec_batch) // SUBLANES) * SUBLANES
    cond = jnp.zeros((n_rows, d), F32).at[0].set(c_ctx).at[1:1 + dec_batch].set(c)
    mod_all = ada_modulation(cond, w_ada, b_ada)

    tm_row = _tile(math.gcd(n1, seq2), 256)
    tm_prep = _tile(math.gcd(n1, seq2), 512)
    m_cap = math.gcd(n1, seq2)
    cos, sin = _rope_tables(seq2, hd, tm_prep)

    w_in_t = jnp.swapaxes(w_in, 1, 2)
    h = (x_prompt.reshape(n1, d), x_sample.reshape(n2, d))
    new_k, new_v, new_s, xs_buf = [], [], None, None
    for l in range(depth):
        mod = mod_all[l].reshape(n_rows, 1, N_MOD * d)
        w_lr = jnp.zeros((d, LANES), F32).at[:, :2 * rank].set(w_in_t[l, main_cols:, :].T)
        up = jnp.zeros((LANES, 2 * key_w), F32)
        up = up.at[:rank, :key_w].set(gla_up[l, 0]).at[rank:2 * rank, key_w:].set(gla_up[l, 1])
        bias = gla_bias[l].reshape(1, 2 * key_w)

        u, la = norm_modulate(h, norm1_w[l], mod, 1, 0, row_group, tm_row, mode="decay", extra=(w_lr, up, bias))
        z = matmul(u, [w_in_t], l, mode="plain", out_dtype=BF16, m_cap=m_cap, n_cols=main_cols, w_transposed=True)

        qk, kn = qk_prepare(z, q_norm_w[l], k_norm_w[l], cos, sin, table_block, tm_prep, n_q=n_q, n_kv=n_kv, hd=hd)
        new_k.append(kn[:n1].reshape(batch, seq1, n_kv, hd))
        new_v.append(z[:n1, attn_w + kv_w:attn_w + 2 * kv_w].astype(F32).reshape(batch, seq1, n_kv, hd))

        attn_kw = dict(n_q=n_q, n_kv=n_kv, hd=hd, v_col0=attn_w + kv_w, mix_shape=(n, attn_w + gla_w))
        mix = attention(qk, z, 0, batch, seq1, mix=None, **attn_kw)
        mix = attention(qk, z, n1, dec_batch, seq2, mix=mix, ctx=(cache_k, cache_v, l), **attn_kw)
        gla_kw = dict(heads=heads, dk=dk, dv=dv, q_col0=q_col0, k_col0=k_col0, v_col0=v_col0, g_col0=g_col0)
        mix, new_s = gla(z, la, gla_norm_w[l], 0, batch, seq1, mix, attn_w, new_state=(new_s, l, depth), **gla_kw)
        mix, _ = gla(z, la, gla_norm_w[l], n1, dec_batch, seq2, mix, attn_w, s0=(state_gla, l), **gla_kw)

        h = matmul(mix, [w_out], l, mode="resid", out_dtype=F32, m_cap=m_cap, h=h, mod=mod, gate_col=2 * d,
                   row_group=row_group)

        i = l // 2
        if l % 2 == 0:
            (u2,) = norm_modulate(h, norm2_w[l], mod, 4, 3, row_group, tm_row, mode="plain", extra=())
            hid = matmul(u2, [ffn_w1, ffn_w3], i, mode="swiglu", out_dtype=BF16, m_cap=m_cap)
            h = matmul(hid, [ffn_w2], i, mode="resid", out_dtype=F32, m_cap=m_cap, h=h, mod=mod, gate_col=5 * d,
                       row_group=row_group)
        else:
            rw = jnp.zeros((d, LANES), F32).at[:, :n_experts].set(router_w[i])
            xp, meta, counts = norm_modulate(h, norm2_w[l], mod, 4, 3, row_group, tm_row, mode="router",
                                             extra=(rw,), n_experts=n_experts)
            h, xs_buf = moe_ffn(xp, meta, counts, h, mod, 5 * d, row_group, moe_w1, moe_w3, moe_w2, i, tm_row,
                                tm_prep, xs_buf)

    y_prompt = h[:n1].reshape(batch, seq1, d)
    y_sample = h[n1:].reshape(dec_batch, seq2, d)
    return (y_prompt, y_sample, jnp.stack(new_k, axis=1), jnp.stack(new_v, axis=1),
            new_s.astype(x_prompt.dtype))
```
